```python
import math
import jax, jax.numpy as jnp
from jax import lax
import numpy as np

D_MODEL = 2048
BATCH = 2
SEQ = 4096
DEPTH = 4
DEC_BATCH = 8
DEC_SEQ = 4
PAST_LEN = 16384
PAGE_SIZE = 128

N_A_LAYERS = DEPTH // 2
N_B_LAYERS = DEPTH - N_A_LAYERS
LRU_WIDTH = D_MODEL
N_GATE_BLOCKS = 8
GATE_BLOCK = LRU_WIDTH // N_GATE_BLOCKS
CONV_WIDTH = 4
RG_C = 8.0
N_HEADS = 16
HEAD_DIM = 64
V_DIM = 2 * HEAD_DIM
QK_WIDTH = N_HEADS * 2 * HEAD_DIM
V_WIDTH = N_HEADS * V_DIM
D_FF = ((-(-8 * D_MODEL // 3) + 255) // 256) * 256
ROPE_THETA = 10000.0
EPS = 1e-6
Q_BLOCK = 128

kernel_name = 'yoco_griffin_diff_attn_step'


def rmsnorm(x, g):
    x32 = x.astype(jnp.float32)
    y = x32 * lax.rsqrt(jnp.mean(x32 * x32, axis=-1, keepdims=True) + EPS)
    return y.astype(x.dtype) * g


def rope(x, pos):
    half = HEAD_DIM // 2
    inv = ROPE_THETA ** (-jnp.arange(half, dtype=jnp.float32) / half)
    ang = pos.astype(jnp.float32)[:, None] * inv[None, :]
    cos = jnp.cos(ang)[None, :, None, None, :]
    sin = jnp.sin(ang)[None, :, None, None, :]
    x32 = x.astype(jnp.float32)
    x1, x2 = x32[..., :half], x32[..., half:]
    return jnp.concatenate([x1 * cos - x2 * sin, x2 * cos + x1 * sin], axis=-1).astype(x.dtype)


def causal_conv(u, buf, w, b):
    T = u.shape[1]
    up = jnp.concatenate([buf, u], axis=1)
    out = b + up[:, CONV_WIDTH - 1:CONV_WIDTH - 1 + T] * w[0]
    for j in range(1, CONV_WIDTH):
        out = out + up[:, CONV_WIDTH - 1 - j:CONV_WIDTH - 1 - j + T] * w[j]
    return out, up[:, T:]


def block_diag(x, w, b):
    B, T, W = x.shape
    xb = x.reshape(B, T, N_GATE_BLOCKS, GATE_BLOCK)
    return jnp.einsum('btnc,ncd->btnd', xb, w).reshape(B, T, W) + b


def rg_lru(x, h0, w_a, b_a, w_i, b_i, lam):
    r = jax.nn.sigmoid(block_diag(x, w_a, b_a).astype(jnp.float32))
    i = jax.nn.sigmoid(block_diag(x, w_i, b_i).astype(jnp.float32))
    log_a = -RG_C * r * jax.nn.softplus(-lam.astype(jnp.float32))
    a = jnp.exp(log_a)
    mult = jnp.sqrt(-jnp.expm1(2.0 * log_a))
    bterm = mult * i * x.astype(jnp.float32)
    bterm = bterm.at[:, 0].add(a[:, 0] * h0.astype(jnp.float32))

    def combine(left, right):
        a1, b1 = left
        a2, b2 = right
        return a1 * a2, a2 * b1 + b2

    _, h = lax.associative_scan(combine, (a, bterm), axis=1)
    return h.astype(x.dtype), h[:, -1].astype(x.dtype)


def recurrent_block(xn, conv_buf, h0, w_x, w_gate, conv_w, conv_b, w_a, b_a, w_i, b_i, lam, w_out):
    gate = jax.nn.gelu(xn @ w_gate)
    u = xn @ w_x
    u, new_buf = causal_conv(u, conv_buf, conv_w, conv_b)
    h, h_last = rg_lru(u, h0, w_a, b_a, w_i, b_i, lam)
    return (h * gate) @ w_out, new_buf, h_last


def shared_kv(x, pos, kv_norm, w_kv):
    B, T, _ = x.shape
    kv = rmsnorm(x, kv_norm) @ w_kv
    k = rope(kv[..., :QK_WIDTH].reshape(B, T, N_HEADS, 2, HEAD_DIM), pos)
    v = kv[..., QK_WIDTH:].reshape(B, T, N_HEADS, V_DIM)
    return k, v


def diff_attend(q, k, v, q_pos, k_pos, lam):
    s = jnp.einsum('bqhcd,bkhcd->bhcqk', q, k).astype(jnp.float32) * (HEAD_DIM ** -0.5)
    mask = k_pos[None, :] <= q_pos[:, None]
    s = jnp.where(mask[None, None, None], s, -jnp.inf)
    p = jax.nn.softmax(s, axis=-1)
    a = p[:, :, 0] - lam * p[:, :, 1]
    return jnp.einsum('bhqk,bkhd->bqhd', a.astype(v.dtype), v)


def diff_attention(q, k, v, q_pos, k_pos, lam):
    B, T = q.shape[0], q.shape[1]
    if T <= Q_BLOCK or T % Q_BLOCK != 0:
        return diff_attend(q, k, v, q_pos, k_pos, lam)
    nb = T // Q_BLOCK
    qb = q.reshape(B, nb, Q_BLOCK, N_HEADS, 2, HEAD_DIM).swapaxes(0, 1)
    pb = q_pos.reshape(nb, Q_BLOCK)
    out = lax.map(lambda args: diff_attend(args[0], k, v, args[1], k_pos, lam), (qb, pb))
    return out.swapaxes(0, 1).reshape(B, T, N_HEADS, V_DIM)


def diff_block(xn, pos, k_all, v_all, k_pos, w_q, lq1, lk1, lq2, lk2, subln, w_o, lam_init):
    B, T, _ = xn.shape
    q = rope((xn @ w_q).reshape(B, T, N_HEADS, 2, HEAD_DIM), pos)
    lam = (jnp.exp(jnp.sum(lq1.astype(jnp.float32) * lk1.astype(jnp.float32)))
           - jnp.exp(jnp.sum(lq2.astype(jnp.float32) * lk2.astype(jnp.float32))) + lam_init)
    o = diff_attention(q, k_all, v_all, pos, k_pos, lam)
    o = rmsnorm(o, subln) * (1.0 - lam_init)
    return o.reshape(B, T, V_WIDTH) @ w_o


def swiglu(xn, w_gate, w_up, w_down):
    return (jax.nn.silu(xn @ w_gate) * (xn @ w_up)) @ w_down


def trunk(x, pos, conv_bufs, h0s, past_k, past_v, p):
    new_bufs, new_hs = [], []
    k_new = v_new = k_all = v_all = k_pos = None
    for l in range(DEPTH):
        xn = rmsnorm(x, p['norm_mix'][l])
        if l < N_A_LAYERS:
            y, nb, nh = recurrent_block(xn, conv_bufs[l], h0s[l], p['rg_w_x'][l], p['rg_w_gate'][l],
                                        p['rg_conv_w'][l], p['rg_conv_b'][l], p['rg_w_a'][l], p['rg_b_a'][l],
                                        p['rg_w_i'][l], p['rg_b_i'][l], p['rg_lambda'][l], p['rg_w_out'][l])
            new_bufs.append(nb)
            new_hs.append(nh)
        else:
            j = l - N_A_LAYERS
            if j == 0:
                k_new, v_new = shared_kv(x, pos, p['kv_norm'], p['w_kv'])
                if past_k is None:
                    k_all, v_all, k_pos = k_new, v_new, pos
                else:
                    k_all = jnp.concatenate([past_k, k_new], axis=1)
                    v_all = jnp.concatenate([past_v, v_new], axis=1)
                    k_pos = jnp.concatenate([jnp.arange(past_k.shape[1], dtype=jnp.int32), pos])
            lam_init = 0.8 - 0.6 * math.exp(-0.3 * l)
            y = diff_block(xn, pos, k_all, v_all, k_pos, p['dif_w_q'][j], p['dif_lq1'][j], p['dif_lk1'][j],
                           p['dif_lq2'][j], p['dif_lk2'][j], p['dif_subln'][j], p['dif_w_o'][j], lam_init)
        x = x + y
        x = x + swiglu(rmsnorm(x, p['norm_ffn'][l]), p['ffn_w_gate'][l], p['ffn_w_up'][l], p['ffn_w_down'][l])
    return rmsnorm(x, p['norm_final']), k_new, v_new, jnp.stack(new_bufs), jnp.stack(new_hs)


def setup_inputs(seed: int = 0) -> dict:
    key = jax.random.key(seed)
    ks = iter(jax.random.split(key, 48))

    def nrm(shape, scale):
        return jax.random.normal(next(ks), shape, jnp.float32) * scale

    n_pages = PAST_LEN // PAGE_SIZE
    n_phys = (5 * DEC_BATCH * n_pages) // 4
    page_table = jax.random.permutation(next(ks), n_phys)[:DEC_BATCH * n_pages]
    page_table = page_table.reshape(DEC_BATCH, n_pages).astype(jnp.int32)
    u = jax.random.uniform(next(ks), (N_A_LAYERS, LRU_WIDTH), jnp.float32, 0.9, 0.999)
    a0 = u ** (1.0 / RG_C)
    rg_lambda = jnp.log(a0) - jnp.log1p(-a0)
    out_scale = (2 * DEPTH) ** -0.5
    return {
        'x_prompt': nrm((BATCH, SEQ, D_MODEL), 1.0),
        'x_sample': nrm((DEC_BATCH, DEC_SEQ, D_MODEL), 1.0),
        'cache_k': nrm((n_phys, PAGE_SIZE, N_HEADS, 2, HEAD_DIM), 1.0),
        'cache_v': nrm((n_phys, PAGE_SIZE, N_HEADS, V_DIM), 1.0),
        'page_table': page_table,
        'state_conv': nrm((N_A_LAYERS, DEC_BATCH, CONV_WIDTH - 1, LRU_WIDTH), 0.5),
        'state_rglru': nrm((N_A_LAYERS, DEC_BATCH, LRU_WIDTH), 0.5),
        'norm_mix': 1.0 + nrm((DEPTH, D_MODEL), 0.02),
        'norm_ffn': 1.0 + nrm((DEPTH, D_MODEL), 0.02),
        'norm_final': 1.0 + nrm((D_MODEL,), 0.02),
        'rg_w_x': nrm((N_A_LAYERS, D_MODEL, LRU_WIDTH), D_MODEL ** -0.5),
        'rg_w_gate': nrm((N_A_LAYERS, D_MODEL, LRU_WIDTH), D_MODEL ** -0.5),
        'rg_conv_w': nrm((N_A_LAYERS, CONV_WIDTH, LRU_WIDTH), CONV_WIDTH ** -0.5),
        'rg_conv_b': nrm((N_A_LAYERS, LRU_WIDTH), 0.01),
        'rg_w_a': nrm((N_A_LAYERS, N_GATE_BLOCKS, GATE_BLOCK, GATE_BLOCK), GATE_BLOCK ** -0.5),
        'rg_b_a': nrm((N_A_LAYERS, LRU_WIDTH), 0.01),
        'rg_w_i': nrm((N_A_LAYERS, N_GATE_BLOCKS, GATE_BLOCK, GATE_BLOCK), GATE_BLOCK ** -0.5),
        'rg_b_i': nrm((N_A_LAYERS, LRU_WIDTH), 0.01),
        'rg_lambda': rg_lambda,
        'rg_w_out': nrm((N_A_LAYERS, LRU_WIDTH, D_MODEL), LRU_WIDTH ** -0.5 * out_scale),
        'kv_norm': 1.0 + nrm((D_MODEL,), 0.02),
        'w_kv': nrm((D_MODEL, QK_WIDTH + V_WIDTH), D_MODEL ** -0.5),
        'dif_w_q': nrm((N_B_LAYERS, D_MODEL, QK_WIDTH), D_MODEL ** -0.5),
        'dif_lq1': nrm((N_B_LAYERS, HEAD_DIM), 0.1),
        'dif_lk1': nrm((N_B_LAYERS, HEAD_DIM), 0.1),
        'dif_lq2': nrm((N_B_LAYERS, HEAD_DIM), 0.1),
        'dif_lk2': nrm((N_B_LAYERS, HEAD_DIM), 0.1),
        'dif_subln': 1.0 + nrm((N_B_LAYERS, V_DIM), 0.02),
        'dif_w_o': nrm((N_B_LAYERS, V_WIDTH, D_MODEL), V_WIDTH ** -0.5 * out_scale),
        'ffn_w_gate': nrm((DEPTH, D_MODEL, D_FF), D_MODEL ** -0.5),
        'ffn_w_up': nrm((DEPTH, D_MODEL, D_FF), D_MODEL ** -0.5),
        'ffn_w_down': nrm((DEPTH, D_FF, D_MODEL), D_FF ** -0.5 * out_scale),
    }


def reference(x_prompt, x_sample, cache_k, cache_v, page_table, state_conv, state_rglru,
              norm_mix, norm_ffn, norm_final, rg_w_x, rg_w_gate, rg_conv_w, rg_conv_b,
              rg_w_a, rg_b_a, rg_w_i, rg_b_i, rg_lambda, rg_w_out, kv_norm, w_kv,
              dif_w_q, dif_lq1, dif_lk1, dif_lq2, dif_lk2, dif_subln, dif_w_o,
              ffn_w_gate, ffn_w_up, ffn_w_down):
    p = dict(norm_mix=norm_mix, norm_ffn=norm_ffn, norm_final=norm_final, rg_w_x=rg_w_x,
             rg_w_gate=rg_w_gate, rg_conv_w=rg_conv_w, rg_conv_b=rg_conv_b, rg_w_a=rg_w_a,
             rg_b_a=rg_b_a, rg_w_i=rg_w_i, rg_b_i=rg_b_i, rg_lambda=rg_lambda, rg_w_out=rg_w_out,
             kv_norm=kv_norm, w_kv=w_kv, dif_w_q=dif_w_q, dif_lq1=dif_lq1, dif_lk1=dif_lk1,
             dif_lq2=dif_lq2, dif_lk2=dif_lk2, dif_subln=dif_subln, dif_w_o=dif_w_o,
             ffn_w_gate=ffn_w_gate, ffn_w_up=ffn_w_up, ffn_w_down=ffn_w_down)

    b_p, t_p = x_prompt.shape[0], x_prompt.shape[1]
    pos_p = jnp.arange(t_p, dtype=jnp.int32)
    conv0 = jnp.zeros((N_A_LAYERS, b_p, CONV_WIDTH - 1, LRU_WIDTH), x_prompt.dtype)
    h0 = jnp.zeros((N_A_LAYERS, b_p, LRU_WIDTH), x_prompt.dtype)
    y_prompt, k_prompt, v_prompt, conv_prompt, h_prompt = trunk(x_prompt, pos_p, conv0, h0, None, None, p)

    b_s, t_s = x_sample.shape[0], x_sample.shape[1]
    past_len = page_table.shape[1] * cache_k.shape[1]
    k_past = cache_k[page_table].reshape(b_s, past_len, N_HEADS, 2, HEAD_DIM)
    v_past = cache_v[page_table].reshape(b_s, past_len, N_HEADS, V_DIM)
    pos_s = past_len + jnp.arange(t_s, dtype=jnp.int32)
    y_sample, k_sample, v_sample, conv_sample, h_sample = trunk(x_sample, pos_s, state_conv, state_rglru,
                                                                k_past, v_past, p)
    return (y_prompt, y_sample, k_prompt, v_prompt, conv_prompt, h_prompt,
            k_sample, v_sample, conv_sample, h_sample)
```

```python
import functools
import math

import jax
import jax.numpy as jnp
from jax import lax
from jax.experimental import pallas as pl
from jax.experimental.pallas import tpu as pltpu

F32 = jnp.float32
BF16 = jnp.bfloat16

EPS = 1e-6
RG_C = 8.0
ROPE_THETA = 10000.0
CONV_WIDTH = 4
LANES = 128
SUBLANES = 8
NEG_BIG = -1e30
VMEM_LIMIT = 56 * 1024 * 1024


def _cparams(sem):
    return pltpu.CompilerParams(dimension_semantics=sem, vmem_limit_bytes=VMEM_LIMIT)


def _nmm_kernel(*refs, n_w, n_extra, n_out, epi):
    x_ref, g_ref = refs[0], refs[1]
    w_refs = refs[2:2 + n_w]
    e_refs = refs[2 + n_w:2 + n_w + n_extra]
    o_refs = refs[2 + n_w + n_extra:2 + n_w + n_extra + n_out]
    xn_ref = refs[-1]

    @pl.when(pl.program_id(1) == 0)
    def _():
        x = x_ref[...]
        ms = jnp.mean(x * x, axis=-1, keepdims=True)
        xn_ref[...] = ((x * lax.rsqrt(ms + EPS)) * g_ref[...]).astype(BF16)

    xn = xn_ref[...]
    accs = [jnp.dot(xn, w[...], preferred_element_type=F32) for w in w_refs]
    epi(accs, e_refs, o_refs)


def _norm_matmul(x, g, ws, extras, out_dtypes, epi, *, tm, tn, name):
    m, d = x.shape
    n = ws[0].shape[1]
    tm = min(tm, m)
    grid = (m // tm, n // tn)
    in_specs = [pl.BlockSpec((tm, d), lambda i, j: (i, 0)),
                pl.BlockSpec((1, d), lambda i, j: (0, 0))]
    in_specs += [pl.BlockSpec((d, tn), lambda i, j: (0, j)) for _ in ws]
    in_specs += [pl.BlockSpec((tm, LANES), lambda i, j: (i, 0)) for _ in extras]
    out_specs = [pl.BlockSpec((tm, tn), lambda i, j: (i, j)) for _ in out_dtypes]
    out_shape = [jax.ShapeDtypeStruct((m, n), dt) for dt in out_dtypes]
    kern = functools.partial(_nmm_kernel, n_w=len(ws), n_extra=len(extras), n_out=len(out_dtypes), epi=epi)
    return pl.pallas_call(
        kern, grid=grid, in_specs=in_specs, out_specs=out_specs, out_shape=out_shape,
        scratch_shapes=[pltpu.VMEM((tm, d), BF16)],
        compiler_params=_cparams(("parallel", "arbitrary")), name=name,
    )(x, g.reshape(1, d), *ws, *extras)


def _epi_recurrent_in(accs, e_refs, o_refs):
    o_refs[0][...] = jax.nn.gelu(accs[0])
    o_refs[1][...] = accs[1]


def _epi_swiglu(accs, e_refs, o_refs):
    o_refs[0][...] = (jax.nn.silu(accs[0]) * accs[1]).astype(BF16)


def _rope_slab(y, cos, sin_signed):
    lane = lax.broadcasted_iota(jnp.int32, y.shape, 1)
    first_half = (lane % 64) < 32
    rot = jnp.where(first_half, pltpu.roll(y, LANES - 32, 1), pltpu.roll(y, 32, 1))
    return y * cos + rot * sin_signed


def _epi_q(accs, e_refs, o_refs, *, scale):
    cos, sin_signed = e_refs[0][...], e_refs[1][...]
    for s in range(accs[0].shape[1] // LANES):
        sl = slice(s * LANES, (s + 1) * LANES)
        o_refs[0][:, sl] = (_rope_slab(accs[0][:, sl], cos, sin_signed) * scale).astype(BF16)


def _epi_kv(accs, e_refs, o_refs):
    cos, sin_signed = e_refs[0][...], e_refs[1][...]
    for s in range(accs[0].shape[1] // LANES):
        sl = slice(s * LANES, (s + 1) * LANES)
        k = _rope_slab(accs[0][:, sl], cos, sin_signed)
        o_refs[0][:, sl] = k
        o_refs[1][:, sl] = k.astype(BF16)
    o_refs[2][...] = accs[1]
    o_refs[3][...] = accs[1].astype(BF16)


def _mm_res_kernel(a_ref, w_ref, r_ref, o_ref):
    o_ref[...] = r_ref[...] + jnp.dot(a_ref[...], w_ref[...], preferred_element_type=F32)


def _matmul_residual(a, w, res, *, tm, tn, name):
    m, k = a.shape
    n = w.shape[1]
    tm = min(tm, m)
    return pl.pallas_call(
        _mm_res_kernel, grid=(m // tm, n // tn),
        in_specs=[pl.BlockSpec((tm, k), lambda i, j: (i, 0)),
                  pl.BlockSpec((k, tn), lambda i, j: (0, j)),
                  pl.BlockSpec((tm, tn), lambda i, j: (i, j))],
        out_specs=pl.BlockSpec((tm, tn), lambda i, j: (i, j)),
        out_shape=jax.ShapeDtypeStruct((m, n), F32),
        compiler_params=_cparams(("parallel", "arbitrary")), name=name,
    )(a, w, res)


def _rmsnorm_kernel(x_ref, g_ref, o_ref):
    x = x_ref[...]
    ms = jnp.mean(x * x, axis=-1, keepdims=True)
    o_ref[...] = (x * lax.rsqrt(ms + EPS)) * g_ref[...]


def _rmsnorm(x, g, *, tm):
    m, d = x.shape
    tm = min(tm, m)
    return pl.pallas_call(
        _rmsnorm_kernel, grid=(m // tm,),
        in_specs=[pl.BlockSpec((tm, d), lambda i: (i, 0)), pl.BlockSpec((1, d), lambda i: (0, 0))],
        out_specs=pl.BlockSpec((tm, d), lambda i: (i, 0)),
        out_shape=jax.ShapeDtypeStruct((m, d), F32),
        compiler_params=_cparams(("parallel",)), name="final_rmsnorm",
    )(x, g.reshape(1, d))


_EXPM1_SERIES_BOUND = 0.25
_EXPM1_SERIES_TERMS = 8


def _neg_expm1(x):
    poly = jnp.full_like(x, 1.0 / math.factorial(_EXPM1_SERIES_TERMS))
    for k in range(_EXPM1_SERIES_TERMS - 1, 0, -1):
        poly = poly * x + 1.0 / math.factorial(k)
    return jnp.where(x > -_EXPM1_SERIES_BOUND, -(poly * x), 1.0 - jnp.exp(x))


def _scan_kernel(u_ref, gate_ref, cw_ref, cb_ref, wa_ref, ba_ref, wi_ref, bi_ref, lam_ref, conv0_ref, h0_ref,
                 hg_ref, convout_ref, hlast_ref, ubuf, a_s, b_s, hs, hcar, *, tc, last_row, n_gate_blocks):
    c = pl.program_id(1)
    width = u_ref.shape[2]
    pad = SUBLANES
    hist = CONV_WIDTH - 1

    @pl.when(c == 0)
    def _():
        ubuf[0:pad, :] = jnp.zeros((pad, width), F32)
        ubuf[pad - hist:pad, :] = conv0_ref[0]
        hcar[...] = jnp.broadcast_to(h0_ref[0], (SUBLANES, width))

    u = u_ref[0]
    ubuf[pad:pad + tc, :] = u
    cw = cw_ref[...]
    xc = cb_ref[...] + u * cw[0:1, :]
    for j in range(1, CONV_WIDTH):
        xc = xc + ubuf[pad - j:pad - j + tc, :] * cw[j:j + 1, :]

    neg_c_softplus = -RG_C * jax.nn.softplus(-lam_ref[...])
    gb = width // n_gate_blocks
    for n in range(n_gate_blocks):
        sl = slice(n * gb, (n + 1) * gb)
        xs = xc[:, sl]
        xb = xs.astype(BF16)
        r = jax.nn.sigmoid(jnp.dot(xb, wa_ref[n], preferred_element_type=F32) + ba_ref[:, sl])
        i = jax.nn.sigmoid(jnp.dot(xb, wi_ref[n], preferred_element_type=F32) + bi_ref[:, sl])
        log_a = r * neg_c_softplus[:, sl]
        a_s[:, sl] = jnp.exp(log_a)
        b_s[:, sl] = jnp.sqrt(_neg_expm1(2.0 * log_a)) * i * xs

    row = lax.broadcasted_iota(jnp.int32, (SUBLANES, width), 0)

    def group(g, h):
        base = pl.multiple_of(g * SUBLANES, SUBLANES)
        a8 = a_s[pl.ds(base, SUBLANES), :]
        b8 = b_s[pl.ds(base, SUBLANES), :]
        out = jnp.zeros((SUBLANES, width), F32)
        for r in range(SUBLANES):
            hn = a8 * h + b8
            out = jnp.where(row == r, hn, out)
            h = jnp.broadcast_to(hn[r:r + 1, :], (SUBLANES, width))
        hs[pl.ds(base, SUBLANES), :] = out
        return h

    hcar[...] = lax.fori_loop(0, tc // SUBLANES, group, hcar[...])
    hg_ref[0] = (hs[...] * gate_ref[0]).astype(BF16)
    ubuf[0:pad, :] = ubuf[tc:tc + pad, :]

    @pl.when(c == pl.num_programs(1) - 1)
    def _():
        convout_ref[0] = ubuf[pad + last_row + 1 - hist:pad + last_row + 1, :]
        hlast_ref[0] = hs[last_row:last_row + 1, :]


def _conv_rglru(u, gate, conv0, h0, cw, cb, wa, ba, wi, bi, lam, *, tc, t_valid):
    b, t, width = u.shape
    tc = min(tc, t)
    n_chunks = t // tc
    last_row = (t_valid - 1) - (n_chunks - 1) * tc
    nb = wa.shape[0]
    row2 = lambda v: v.reshape(1, width)
    kern = functools.partial(_scan_kernel, tc=tc, last_row=last_row, n_gate_blocks=nb)
    full2 = lambda shape: pl.BlockSpec(shape, lambda i, c: (0, 0))
    return pl.pallas_call(
        kern, grid=(b, n_chunks),
        in_specs=[pl.BlockSpec((1, tc, width), lambda i, c: (i, c, 0)),
                  pl.BlockSpec((1, tc, width), lambda i, c: (i, c, 0)),
                  full2((CONV_WIDTH, width)), full2((1, width)),
                  pl.BlockSpec(wa.shape, lambda i, c: (0, 0, 0)), full2((1, width)),
                  pl.BlockSpec(wi.shape, lambda i, c: (0, 0, 0)), full2((1, width)),
                  full2((1, width)),
                  pl.BlockSpec((1, CONV_WIDTH - 1, width), lambda i, c: (i, 0, 0)),
                  pl.BlockSpec((1, 1, width), lambda i, c: (i, 0, 0))],
        out_specs=[pl.BlockSpec((1, tc, width), lambda i, c: (i, c, 0)),
                   pl.BlockSpec((1, CONV_WIDTH - 1, width), lambda i, c: (i, 0, 0)),
                   pl.BlockSpec((1, 1, width), lambda i, c: (i, 0, 0))],
        out_shape=[jax.ShapeDtypeStruct((b, t, width), BF16),
                   jax.ShapeDtypeStruct((b, CONV_WIDTH - 1, width), F32),
                   jax.ShapeDtypeStruct((b, 1, width), F32)],
        scratch_shapes=[pltpu.VMEM((tc + SUBLANES, width), F32),
                        pltpu.VMEM((tc, width), F32), pltpu.VMEM((tc, width), F32),
                        pltpu.VMEM((tc, width), F32), pltpu.VMEM((SUBLANES, width), F32)],
        compiler_params=_cparams(("parallel", "arbitrary")), name="conv_rglru",
    )(u, gate, cw, row2(cb), wa, row2(ba), wi, row2(bi), row2(lam), conv0, h0.reshape(b, 1, width))


def _diff_lambda(lq1_ref, lk1_ref, lq2_ref, lk2_ref, lam_init):
    e1 = jnp.exp(jnp.sum(lq1_ref[...] * lk1_ref[...], axis=-1, keepdims=True))
    e2 = jnp.exp(jnp.sum(lq2_ref[...] * lk2_ref[...], axis=-1, keepdims=True))
    return e1 - e2 + lam_init


def _online_softmax_step(s, v, m_ref, l_ref, acc_ref):
    m_old = m_ref[...]
    m_new = jnp.maximum(m_old, jnp.max(s, axis=-1, keepdims=True))
    alpha = jnp.exp(m_old - m_new)
    p = jnp.exp(s - m_new)
    l_ref[...] = alpha * l_ref[...] + jnp.sum(p, axis=-1, keepdims=True)
    acc_ref[...] = alpha * acc_ref[...] + jnp.dot(p.astype(BF16), v, preferred_element_type=F32)
    m_ref[...] = m_new


def _head_out(n0, n1, lam, subln, out_scale):
    o = n0 - lam * n1
    ms = jnp.mean(o * o, axis=-1, keepdims=True)
    return ((o * lax.rsqrt(ms + EPS)) * subln) * out_scale


def _attn_prompt_kernel(lq1_ref, lk1_ref, lq2_ref, lk2_ref, subln_ref, q_ref, k_ref, v_ref, o_ref,
                        qs_ref, m_ref, l_ref, acc_ref, *, tq, tk, lam_init):
    qi = pl.program_id(2)
    q = q_ref[0]
    lane = lax.broadcasted_iota(jnp.int32, q.shape, 1)
    zero = jnp.zeros_like(q)
    qs_ref[0:tq, :] = jnp.where(lane < 64, q, zero)
    qs_ref[tq:2 * tq, :] = jnp.where(lane >= 64, q, zero)
    m_ref[...] = jnp.full(m_ref.shape, NEG_BIG, F32)
    l_ref[...] = jnp.zeros(l_ref.shape, F32)
    acc_ref[...] = jnp.zeros(acc_ref.shape, F32)

    def scores(ki):
        start = pl.multiple_of(ki * tk, tk)
        kt = k_ref[0, pl.ds(start, tk), :]
        vt = v_ref[0, pl.ds(start, tk), :]
        s = lax.dot_general(qs_ref[...], kt, (((1,), (1,)), ((), ())), preferred_element_type=F32)
        return s, vt

    def body(ki, carry):
        s, vt = scores(ki)
        _online_softmax_step(s, vt, m_ref, l_ref, acc_ref)
        return carry

    n_full = (qi * tq) // tk
    lax.fori_loop(0, n_full, body, 0)

    s, vt = scores(n_full)
    qpos = qi * tq + lax.broadcasted_iota(jnp.int32, (2 * tq, tk), 0) % tq
    kpos = n_full * tk + lax.broadcasted_iota(jnp.int32, (2 * tq, tk), 1)
    s = jnp.where(kpos <= qpos, s, NEG_BIG)
    _online_softmax_step(s, vt, m_ref, l_ref, acc_ref)

    lam = _diff_lambda(lq1_ref, lk1_ref, lq2_ref, lk2_ref, lam_init)
    n = acc_ref[...] / l_ref[...]
    o_ref[0] = _head_out(n[0:tq], n[tq:2 * tq], lam, subln_ref[...], 1.0 - lam_init).astype(BF16)


def _attn_prompt(q, k, v, lq1, lk1, lq2, lk2, subln, lam_init, *, tq, tk):
    b, t, hw = q.shape
    nh = hw // LANES
    assert tk % tq == 0 and t % tk == 0
    vec = lambda a: a.reshape(1, -1)
    small = lambda w: pl.BlockSpec((1, w), lambda i, h, j: (0, 0))
    kern = functools.partial(_attn_prompt_kernel, tq=tq, tk=tk, lam_init=lam_init)
    return pl.pallas_call(
        kern, grid=(b, nh, t // tq),
        in_specs=[small(lq1.size), small(lk1.size), small(lq2.size), small(lk2.size), small(LANES),
                  pl.BlockSpec((1, tq, LANES), lambda i, h, j: (i, j, h)),
                  pl.BlockSpec((1, t, LANES), lambda i, h, j: (i, 0, h)),
                  pl.BlockSpec((1, t, LANES), lambda i, h, j: (i, 0, h))],
        out_specs=pl.BlockSpec((1, tq, LANES), lambda i, h, j: (i, j, h)),
        out_shape=jax.ShapeDtypeStruct((b, t, hw), BF16),
        scratch_shapes=[pltpu.VMEM((2 * tq, LANES), BF16), pltpu.VMEM((2 * tq, 1), F32),
                        pltpu.VMEM((2 * tq, 1), F32), pltpu.VMEM((2 * tq, LANES), F32)],
        compiler_params=_cparams(("parallel", "parallel", "arbitrary")), name="diff_attn_prompt",
    )(vec(lq1), vec(lk1), vec(lq2), vec(lk2), vec(subln), q, k, v)


DEC_ROWS = 2 * SUBLANES


def _attn_decode_kernel(pt_ref, lq1_ref, lk1_ref, lq2_ref, lk2_ref, subln_ref, q_ref, kn_ref, vn_ref, *rest,
                        pages, n_heads, t_new, lam_init):
    k_refs = rest[:pages]
    v_refs = rest[pages:2 * pages]
    o_ref = rest[2 * pages]
    m_ref, l_ref, acc_ref = rest[2 * pages + 1:]
    j = pl.program_id(1)

    @pl.when(j == 0)
    def _():
        m_ref[...] = jnp.full(m_ref.shape, NEG_BIG, F32)
        l_ref[...] = jnp.zeros(l_ref.shape, F32)
        acc_ref[...] = jnp.zeros(acc_ref.shape, F32)

    for h in range(n_heads):
        sl = slice(h * LANES, (h + 1) * LANES)
        kt = jnp.concatenate([kr[0, :, sl].astype(BF16) for kr in k_refs], axis=0)
        vt = jnp.concatenate([vr[0, :, sl].astype(BF16) for vr in v_refs], axis=0)
        s = lax.dot_general(q_ref[0, h], kt, (((1,), (1,)), ((), ())), preferred_element_type=F32)
        _online_softmax_step(s, vt, m_ref.at[h], l_ref.at[h], acc_ref.at[h])

    @pl.when(j == pl.num_programs(1) - 1)
    def _():
        lam = _diff_lambda(lq1_ref, lk1_ref, lq2_ref, lk2_ref, lam_init)
        n_new = kn_ref.shape[1]
        qt = lax.broadcasted_iota(jnp.int32, (DEC_ROWS, n_new), 0) % SUBLANES
        kt_pos = lax.broadcasted_iota(jnp.int32, (DEC_ROWS, n_new), 1)
        visible = (kt_pos <= qt) & (kt_pos < t_new)
        for h in range(n_heads):
            sl = slice(h * LANES, (h + 1) * LANES)
            s = lax.dot_general(q_ref[0, h], kn_ref[0, :, sl], (((1,), (1,)), ((), ())),
                                preferred_element_type=F32)
            s = jnp.where(visible, s, NEG_BIG)
            _online_softmax_step(s, vn_ref[0, :, sl], m_ref.at[h], l_ref.at[h], acc_ref.at[h])
            n = acc_ref[h] / l_ref[h]
            o_ref[0, :, sl] = _head_out(n[0:SUBLANES], n[SUBLANES:DEC_ROWS], lam, subln_ref[...], 1.0 - lam_init)


def _attn_decode(qd, k_new, v_new, cache_k, cache_v, page_table, lq1, lk1, lq2, lk2, subln, lam_init, *,
                 pages, t_new):
    b, nh = qd.shape[0], qd.shape[1]
    hw = nh * LANES
    n_pages = page_table.shape[1]
    page = cache_k.shape[1]
    assert n_pages % pages == 0
    vec = lambda a: a.reshape(1, -1)
    small = lambda w: pl.BlockSpec((1, w), lambda i, j, pt: (0, 0))

    def page_spec(p):
        return pl.BlockSpec((1, page, hw), lambda i, j, pt: (pt[i, j * pages + p], 0, 0))

    kern = functools.partial(_attn_decode_kernel, pages=pages, n_heads=nh, t_new=t_new, lam_init=lam_init)
    grid_spec = pltpu.PrefetchScalarGridSpec(
        num_scalar_prefetch=1, grid=(b, n_pages // pages),
        in_specs=[small(lq1.size), small(lk1.size), small(lq2.size), small(lk2.size), small(LANES),
                  pl.BlockSpec((1, nh, DEC_ROWS, LANES), lambda i, j, pt: (i, 0, 0, 0)),
                  pl.BlockSpec((1, k_new.shape[1], hw), lambda i, j, pt: (i, 0, 0)),
                  pl.BlockSpec((1, v_new.shape[1], hw), lambda i, j, pt: (i, 0, 0))]
                 + [page_spec(p) for p in range(pages)] + [page_spec(p) for p in range(pages)],
        out_specs=pl.BlockSpec((1, SUBLANES, hw), lambda i, j, pt: (i, 0, 0)),
        scratch_shapes=[pltpu.VMEM((nh, DEC_ROWS, 1), F32), pltpu.VMEM((nh, DEC_ROWS, 1), F32),
                        pltpu.VMEM((nh, DEC_ROWS, LANES), F32)])
    return pl.pallas_call(
        kern, grid_spec=grid_spec, out_shape=jax.ShapeDtypeStruct((b, SUBLANES, hw), F32),
        compiler_params=_cparams(("parallel", "arbitrary")), name="diff_attn_decode",
    )(page_table, vec(lq1), vec(lk1), vec(lq2), vec(lk2), vec(subln), qd, k_new, v_new,
      *([cache_k] * pages), *([cache_v] * pages))


def _rope_tables(pos, head_dim, batch):
    half = head_dim // 2
    inv = ROPE_THETA ** (-jnp.arange(half, dtype=F32) / half)
    ang = pos.astype(F32)[:, None] * inv[None, :]
    cos, sin = jnp.cos(ang), jnp.sin(ang)
    reps = LANES // head_dim
    cos_t = jnp.tile(jnp.concatenate([cos, cos], axis=1), (batch, reps))
    sin_t = jnp.tile(jnp.concatenate([-sin, sin], axis=1), (batch, reps))
    return cos_t, sin_t


def _trunk(x, pos, conv0, h0, decode, p, cfg):
    b, t, d = x.shape
    m = b * t
    depth = p['norm_mix'].shape[0]
    n_a = p['rg_w_x'].shape[0]
    head_dim = p['dif_lq1'].shape[1]
    v_dim = p['dif_subln'].shape[1]
    qk_width = p['dif_w_q'].shape[2]
    n_heads = qk_width // (2 * head_dim)
    tm, tn = cfg['tm'], cfg['tn']
    x = x.reshape(m, d)
    cos_t, sin_t = _rope_tables(pos, head_dim, b)
    new_bufs, new_hs = [], []
    k_new = v_new = kb = vb = None

    t_pad = -(-t // SUBLANES) * SUBLANES

    for l in range(depth):
        if l < n_a:
            gate, u = _norm_matmul(x, p['norm_mix'][l], [p['rg_w_gate'][l], p['rg_w_x'][l]], [], [F32, F32],
                                   _epi_recurrent_in, tm=tm, tn=tn, name="recurrent_in")
            width = u.shape[1]
            u3, g3 = u.reshape(b, t, width), gate.reshape(b, t, width)
            if t_pad != t:
                u3 = jnp.pad(u3, ((0, 0), (0, t_pad - t), (0, 0)))
                g3 = jnp.pad(g3, ((0, 0), (0, t_pad - t), (0, 0)))
            hg, nb, nh = _conv_rglru(u3, g3, conv0[l], h0[l], p['rg_conv_w'][l], p['rg_conv_b'][l],
                                     p['rg_w_a'][l], p['rg_b_a'][l], p['rg_w_i'][l], p['rg_b_i'][l],
                                     p['rg_lambda'][l], tc=cfg['tc'], t_valid=t)
            new_bufs.append(nb)
            new_hs.append(nh.reshape(b, width))
            hg = hg[:, :t].reshape(m, width)
            x = _matmul_residual(hg, p['rg_w_out'][l], x, tm=tm, tn=tn, name="recurrent_out")
        else:
            j = l - n_a
            if j == 0:
                w_kv = p['w_kv']
                k_new, kb, v_new, vb = _norm_matmul(
                    x, p['kv_norm'], [w_kv[:, :qk_width], w_kv[:, qk_width:]], [cos_t, sin_t],
                    [F32, BF16, F32, BF16], _epi_kv, tm=tm, tn=tn, name="shared_kv")
            lam_init = 0.8 - 0.6 * math.exp(-0.3 * l)
            (q,) = _norm_matmul(x, p['norm_mix'][l], [p['dif_w_q'][j]], [cos_t, sin_t], [BF16],
                                functools.partial(_epi_q, scale=head_dim ** -0.5), tm=tm, tn=tn, name="q_proj")
            lams = (p['dif_lq1'][j], p['dif_lk1'][j], p['dif_lq2'][j], p['dif_lk2'][j], p['dif_subln'][j])
            if decode is None:
                o = _attn_prompt(q.reshape(b, t, qk_width), kb.reshape(b, t, qk_width),
                                 vb.reshape(b, t, n_heads * v_dim), *lams, lam_init,
                                 tq=cfg['tq'], tk=cfg['tk'])
                o = o.reshape(m, n_heads * v_dim)
            else:
                cache_k, cache_v, page_table = decode
                q4 = q.reshape(b, t, n_heads, LANES).transpose(0, 2, 1, 3)
                lane = jnp.arange(LANES)
                zpad = jnp.zeros((b, n_heads, SUBLANES - t, LANES), BF16)
                qd = jnp.concatenate([jnp.where(lane < head_dim, q4, 0).astype(BF16), zpad,
                                      jnp.where(lane >= head_dim, q4, 0).astype(BF16), zpad], axis=2)
                pad_new = lambda a: jnp.pad(a.reshape(b, t, -1), ((0, 0), (0, DEC_ROWS - t), (0, 0)))
                o = _attn_decode(qd, pad_new(kb), pad_new(vb), cache_k, cache_v, page_table, *lams, lam_init,
                                 pages=cfg['pages'], t_new=t)
                o = o[:, :t].reshape(m, n_heads * v_dim).astype(BF16)
            x = _matmul_residual(o, p['dif_w_o'][j], x, tm=tm, tn=tn, name="attn_out")
        (hmid,) = _norm_matmul(x, p['norm_ffn'][l], [p['ffn_w_gate'][l], p['ffn_w_up'][l]], [], [BF16],
                               _epi_swiglu, tm=tm, tn=tn, name="ffn_in")
        x = _matmul_residual(hmid, p['ffn_w_down'][l], x, tm=cfg['tm_down'], tn=tn, name="ffn_down")

    y = _rmsnorm(x, p['norm_final'], tm=cfg['tm_norm']).reshape(b, t, d)
    k_out = k_new.reshape(b, t, n_heads, 2, head_dim)
    v_out = v_new.reshape(b, t, n_heads, v_dim)
    return y, k_out, v_out, jnp.stack(new_bufs), jnp.stack(new_hs)


_MATMUL_WEIGHTS = ('rg_w_x', 'rg_w_gate', 'rg_w_a', 'rg_w_i', 'rg_w_out', 'w_kv', 'dif_w_q', 'dif_w_o',
                   'ffn_w_gate', 'ffn_w_up', 'ffn_w_down')


def kernel(x_prompt, x_sample, cache_k, cache_v, page_table, state_conv, state_rglru, norm_mix, norm_ffn, norm_final, rg_w_x, rg_w_gate, rg_conv_w, rg_conv_b, rg_w_a, rg_b_a, rg_w_i, rg_b_i, rg_lambda, rg_w_out, kv_norm, w_kv, dif_w_q, dif_lq1, dif_lk1, dif_lq2, dif_lk2, dif_subln, dif_w_o, ffn_w_gate, ffn_w_up, ffn_w_down):
    p = dict(norm_mix=norm_mix, norm_ffn=norm_ffn, norm_final=norm_final, rg_w_x=rg_w_x,
             rg_w_gate=rg_w_gate, rg_conv_w=rg_conv_w, rg_conv_b=rg_conv_b, rg_w_a=rg_w_a,
             rg_b_a=rg_b_a, rg_w_i=rg_w_i, rg_b_i=rg_b_i, rg_lambda=rg_lambda, rg_w_out=rg_w_out,
             kv_norm=kv_norm, w_kv=w_kv, dif_w_q=dif_w_q, dif_lq1=dif_lq1, dif_lk1=dif_lk1,
             dif_lq2=dif_lq2, dif_lk2=dif_lk2, dif_subln=dif_subln, dif_w_o=dif_w_o,
             ffn_w_gate=ffn_w_gate, ffn_w_up=ffn_w_up, ffn_w_down=ffn_w_down)
    for name in _MATMUL_WEIGHTS:
        p[name] = p[name].astype(BF16)

    n_a = rg_w_x.shape[0]
    width = rg_w_x.shape[2]
    b_p, t_p = x_prompt.shape[0], x_prompt.shape[1]
    cfg_p = dict(tm=1024, tn=512, tm_down=512, tm_norm=512, tc=256, tq=256, tk=512)
    conv0 = jnp.zeros((n_a, b_p, CONV_WIDTH - 1, width), F32)
    h0 = jnp.zeros((n_a, b_p, width), F32)
    y_p, k_p, v_p, conv_p, h_p = _trunk(x_prompt, jnp.arange(t_p, dtype=jnp.int32), conv0, h0, None, p, cfg_p)

    b_s, t_s = x_sample.shape[0], x_sample.shape[1]
    n_phys, page = cache_k.shape[0], cache_k.shape[1]
    past_len = page_table.shape[1] * page
    cfg_s = dict(tm=b_s * t_s, tn=512, tm_down=b_s * t_s, tm_norm=b_s * t_s, tc=SUBLANES, pages=8)
    decode = (cache_k.reshape(n_phys, page, -1), cache_v.reshape(n_phys, page, -1), page_table)
    pos_s = past_len + jnp.arange(t_s, dtype=jnp.int32)
    y_s, k_s, v_s, conv_s, h_s = _trunk(x_sample, pos_s, state_conv, state_rglru, decode, p, cfg_s)
    return (y_p, y_s, k_p, v_p, conv_p, h_p, k_s, v_s, conv_s, h_s)
```

```python
import functools
import math

import jax
import jax.numpy as jnp
from jax import lax
from jax.experimental import pallas as pl
from jax.experimental.pallas import tpu as pltpu

F32 = jnp.float32
BF16 = jnp.bfloat16

EPS = 1e-6
RG_C = 8.0
ROPE_THETA = 10000.0
CONV_WIDTH = 4
LANES = 128
SUBLANES = 8
NEG_BIG = -1e30
VMEM_LIMIT = 56 * 1024 * 1024


def _cparams(sem):
    return pltpu.CompilerParams(dimension_semantics=sem, vmem_limit_bytes=VMEM_LIMIT)


def _nmm_kernel(*refs, n_w, n_extra, n_out, epi):
    x_ref, g_ref = refs[0], refs[1]
    w_refs = refs[2:2 + n_w]
    e_refs = refs[2 + n_w:2 + n_w + n_extra]
    o_refs = refs[2 + n_w + n_extra:2 + n_w + n_extra + n_out]
    xn_ref = refs[-1]

    @pl.when(pl.program_id(1) == 0)
    def _():
        x = x_ref[...]
        ms = jnp.mean(x * x, axis=-1, keepdims=True)
        xn_ref[...] = ((x * lax.rsqrt(ms + EPS)) * g_ref[...]).astype(BF16)

    xn = xn_ref[...]
    accs = [jnp.dot(xn, w[...], preferred_element_type=F32) for w in w_refs]
    epi(accs, e_refs, o_refs)


def _norm_matmul(x, g, ws, extras, outs, epi, *, tm, tn, name, seq_len=None):
    m, d = x.shape
    n = ws[0].shape[1]
    tm = min(tm, m)
    grid = (m // tm, n // tn)
    in_specs = [pl.BlockSpec((tm, d), lambda i, j: (i, 0)),
                pl.BlockSpec((1, d), lambda i, j: (0, 0))]
    in_specs += [pl.BlockSpec((d, tn), lambda i, j: (0, j)) for _ in ws]
    in_specs += [pl.BlockSpec((tm, LANES), lambda i, j: (i, 0)) for _ in extras]
    out_specs, out_shape = [], []
    for dt, transposed in outs:
        if transposed:
            per_seq = seq_len // tm
            out_specs.append(pl.BlockSpec((1, tn, tm), lambda i, j: (i // per_seq, j, i % per_seq)))
            out_shape.append(jax.ShapeDtypeStruct((m // seq_len, n, seq_len), dt))
        else:
            out_specs.append(pl.BlockSpec((tm, tn), lambda i, j: (i, j)))
            out_shape.append(jax.ShapeDtypeStruct((m, n), dt))
    kern = functools.partial(_nmm_kernel, n_w=len(ws), n_extra=len(extras), n_out=len(outs), epi=epi)
    return pl.pallas_call(
        kern, grid=grid, in_specs=in_specs, out_specs=out_specs, out_shape=out_shape,
        scratch_shapes=[pltpu.VMEM((tm, d), BF16)],
        compiler_params=_cparams(("parallel", "arbitrary")), name=name,
    )(x, g.reshape(1, d), *ws, *extras)


def _epi_recurrent_in(accs, e_refs, o_refs):
    o_refs[0][...] = jax.nn.gelu(accs[0])
    o_refs[1][...] = accs[1]


def _epi_swiglu(accs, e_refs, o_refs):
    o_refs[0][...] = (jax.nn.silu(accs[0]) * accs[1]).astype(BF16)


def _rope_slab(y, cos, sin_signed):
    lane = lax.broadcasted_iota(jnp.int32, y.shape, 1)
    first_half = (lane % 64) < 32
    rot = jnp.where(first_half, pltpu.roll(y, LANES - 32, 1), pltpu.roll(y, 32, 1))
    return y * cos + rot * sin_signed


def _epi_q(accs, e_refs, o_refs, *, scale, transposed):
    cos, sin_signed = e_refs[0][...], e_refs[1][...]
    for s in range(accs[0].shape[1] // LANES):
        sl = slice(s * LANES, (s + 1) * LANES)
        q = _rope_slab(accs[0][:, sl], cos, sin_signed) * scale
        if transposed:
            o_refs[0][0, sl, :] = q.T.astype(BF16)
        else:
            o_refs[0][:, sl] = q.astype(BF16)


def _epi_kv(accs, e_refs, o_refs):
    cos, sin_signed = e_refs[0][...], e_refs[1][...]
    for s in range(accs[0].shape[1] // LANES):
        sl = slice(s * LANES, (s + 1) * LANES)
        k = _rope_slab(accs[0][:, sl], cos, sin_signed)
        o_refs[0][:, sl] = k
        o_refs[1][:, sl] = k.astype(BF16)
    o_refs[2][...] = accs[1]
    o_refs[3][...] = accs[1].astype(BF16)


def _epi_kv_transposed(accs, e_refs, o_refs):
    cos, sin_signed = e_refs[0][...], e_refs[1][...]
    for s in range(accs[0].shape[1] // LANES):
        sl = slice(s * LANES, (s + 1) * LANES)
        k = _rope_slab(accs[0][:, sl], cos, sin_signed)
        o_refs[0][0, sl, :] = k.T
        o_refs[1][:, sl] = k.astype(BF16)
        o_refs[3][0, sl, :] = accs[1][:, sl].T.astype(BF16)
    o_refs[2][...] = accs[1]


def _mm_res_kernel(a_ref, w_ref, r_ref, o_ref):
    o_ref[...] = r_ref[...] + jnp.dot(a_ref[...], w_ref[...], preferred_element_type=F32)


def _matmul_residual(a, w, res, *, tm, tn, name):
    m, k = a.shape
    n = w.shape[1]
    tm = min(tm, m)
    return pl.pallas_call(
        _mm_res_kernel, grid=(m // tm, n // tn),
        in_specs=[pl.BlockSpec((tm, k), lambda i, j: (i, 0)),
                  pl.BlockSpec((k, tn), lambda i, j: (0, j)),
                  pl.BlockSpec((tm, tn), lambda i, j: (i, j))],
        out_specs=pl.BlockSpec((tm, tn), lambda i, j: (i, j)),
        out_shape=jax.ShapeDtypeStruct((m, n), F32),
        compiler_params=_cparams(("parallel", "arbitrary")), name=name,
    )(a, w, res)


def _rmsnorm_kernel(x_ref, g_ref, o_ref):
    x = x_ref[...]
    ms = jnp.mean(x * x, axis=-1, keepdims=True)
    o_ref[...] = (x * lax.rsqrt(ms + EPS)) * g_ref[...]


def _rmsnorm(x, g, *, tm):
    m, d = x.shape
    tm = min(tm, m)
    return pl.pallas_call(
        _rmsnorm_kernel, grid=(m // tm,),
        in_specs=[pl.BlockSpec((tm, d), lambda i: (i, 0)), pl.BlockSpec((1, d), lambda i: (0, 0))],
        out_specs=pl.BlockSpec((tm, d), lambda i: (i, 0)),
        out_shape=jax.ShapeDtypeStruct((m, d), F32),
        compiler_params=_cparams(("parallel",)), name="final_rmsnorm",
    )(x, g.reshape(1, d))


_EXPM1_SERIES_BOUND = 0.25
_EXPM1_SERIES_TERMS = 8


def _neg_expm1(x):
    poly = jnp.full_like(x, 1.0 / math.factorial(_EXPM1_SERIES_TERMS))
    for k in range(_EXPM1_SERIES_TERMS - 1, 0, -1):
        poly = poly * x + 1.0 / math.factorial(k)
    return jnp.where(x > -_EXPM1_SERIES_BOUND, -(poly * x), 1.0 - jnp.exp(x))


def _scan_kernel(u_ref, gate_ref, cw_ref, cb_ref, wa_ref, ba_ref, wi_ref, bi_ref, lam_ref, conv0_ref, h0_ref,
                 hg_ref, convout_ref, hlast_ref, ubuf, a_s, b_s, hs, hcar, *, tc, last_row, n_gate_blocks):
    c = pl.program_id(1)
    width = u_ref.shape[2]
    pad = SUBLANES
    hist = CONV_WIDTH - 1

    @pl.when(c == 0)
    def _():
        ubuf[0:pad, :] = jnp.zeros((pad, width), F32)
        ubuf[pad - hist:pad, :] = conv0_ref[0]
        hcar[...] = jnp.broadcast_to(h0_ref[0], (SUBLANES, width))

    u = u_ref[0]
    ubuf[pad:pad + tc, :] = u
    cw = cw_ref[...]
    xc = cb_ref[...] + u * cw[0:1, :]
    for j in range(1, CONV_WIDTH):
        xc = xc + ubuf[pad - j:pad - j + tc, :] * cw[j:j + 1, :]

    neg_c_softplus = -RG_C * jax.nn.softplus(-lam_ref[...])
    gb = width // n_gate_blocks
    for n in range(n_gate_blocks):
        sl = slice(n * gb, (n + 1) * gb)
        xs = xc[:, sl]
        xb = xs.astype(BF16)
        r = jax.nn.sigmoid(jnp.dot(xb, wa_ref[n], preferred_element_type=F32) + ba_ref[:, sl])
        i = jax.nn.sigmoid(jnp.dot(xb, wi_ref[n], preferred_element_type=F32) + bi_ref[:, sl])
        log_a = r * neg_c_softplus[:, sl]
        a_s[:, sl] = jnp.exp(log_a)
        b_s[:, sl] = jnp.sqrt(_neg_expm1(2.0 * log_a)) * i * xs

    row = lax.broadcasted_iota(jnp.int32, (SUBLANES, width), 0)

    def group(g, h):
        base = pl.multiple_of(g * SUBLANES, SUBLANES)
        a8 = a_s[pl.ds(base, SUBLANES), :]
        b8 = b_s[pl.ds(base, SUBLANES), :]
        out = jnp.zeros((SUBLANES, width), F32)
        for r in range(SUBLANES):
            hn = a8 * h + b8
            out = jnp.where(row == r, hn, out)
            h = jnp.broadcast_to(hn[r:r + 1, :], (SUBLANES, width))
        hs[pl.ds(base, SUBLANES), :] = out
        return h

    hcar[...] = lax.fori_loop(0, tc // SUBLANES, group, hcar[...])
    hg_ref[0] = (hs[...] * gate_ref[0]).astype(BF16)
    ubuf[0:pad, :] = ubuf[tc:tc + pad, :]

    @pl.when(c == pl.num_programs(1) - 1)
    def _():
        convout_ref[0] = ubuf[pad + last_row + 1 - hist:pad + last_row + 1, :]
        hlast_ref[0] = hs[last_row:last_row + 1, :]


def _conv_rglru(u, gate, conv0, h0, cw, cb, wa, ba, wi, bi, lam, *, tc, t_valid):
    b, t, width = u.shape
    tc = min(tc, t)
    n_chunks = t // tc
    last_row = (t_valid - 1) - (n_chunks - 1) * tc
    nb = wa.shape[0]
    row2 = lambda v: v.reshape(1, width)
    kern = functools.partial(_scan_kernel, tc=tc, last_row=last_row, n_gate_blocks=nb)
    full2 = lambda shape: pl.BlockSpec(shape, lambda i, c: (0, 0))
    return pl.pallas_call(
        kern, grid=(b, n_chunks),
        in_specs=[pl.BlockSpec((1, tc, width), lambda i, c: (i, c, 0)),
                  pl.BlockSpec((1, tc, width), lambda i, c: (i, c, 0)),
                  full2((CONV_WIDTH, width)), full2((1, width)),
                  pl.BlockSpec(wa.shape, lambda i, c: (0, 0, 0)), full2((1, width)),
                  pl.BlockSpec(wi.shape, lambda i, c: (0, 0, 0)), full2((1, width)),
                  full2((1, width)),
                  pl.BlockSpec((1, CONV_WIDTH - 1, width), lambda i, c: (i, 0, 0)),
                  pl.BlockSpec((1, 1, width), lambda i, c: (i, 0, 0))],
        out_specs=[pl.BlockSpec((1, tc, width), lambda i, c: (i, c, 0)),
                   pl.BlockSpec((1, CONV_WIDTH - 1, width), lambda i, c: (i, 0, 0)),
                   pl.BlockSpec((1, 1, width), lambda i, c: (i, 0, 0))],
        out_shape=[jax.ShapeDtypeStruct((b, t, width), BF16),
                   jax.ShapeDtypeStruct((b, CONV_WIDTH - 1, width), F32),
                   jax.ShapeDtypeStruct((b, 1, width), F32)],
        scratch_shapes=[pltpu.VMEM((tc + SUBLANES, width), F32),
                        pltpu.VMEM((tc, width), F32), pltpu.VMEM((tc, width), F32),
                        pltpu.VMEM((tc, width), F32), pltpu.VMEM((SUBLANES, width), F32)],
        compiler_params=_cparams(("parallel", "arbitrary")), name="conv_rglru",
    )(u, gate, cw, row2(cb), wa, row2(ba), wi, row2(bi), row2(lam), conv0, h0.reshape(b, 1, width))


def _diff_lambda(lq1_ref, lk1_ref, lq2_ref, lk2_ref, lam_init):
    e1 = jnp.exp(jnp.sum(lq1_ref[...] * lk1_ref[...], axis=-1, keepdims=True))
    e2 = jnp.exp(jnp.sum(lq2_ref[...] * lk2_ref[...], axis=-1, keepdims=True))
    return e1 - e2 + lam_init


def _online_softmax_step(s, v, m_ref, l_ref, acc_ref):
    m_old = m_ref[...]
    m_new = jnp.maximum(m_old, jnp.max(s, axis=-1, keepdims=True))
    alpha = jnp.exp(m_old - m_new)
    p = jnp.exp(s - m_new)
    l_ref[...] = alpha * l_ref[...] + jnp.sum(p, axis=-1, keepdims=True)
    acc_ref[...] = alpha * acc_ref[...] + jnp.dot(p.astype(BF16), v, preferred_element_type=F32)
    m_ref[...] = m_new


def _head_out(n0, n1, lam, subln, out_scale):
    o = n0 - lam * n1
    ms = jnp.mean(o * o, axis=-1, keepdims=True)
    return ((o * lax.rsqrt(ms + EPS)) * subln) * out_scale


def _attn_prompt_kernel(lq1_ref, lk1_ref, lq2_ref, lk2_ref, subln_ref, qt_ref, k_ref, vt_ref, o_ref,
                        qs_ref, m_ref, l_ref, acc_ref, *, tq, lam_init):
    qi = pl.program_id(2)
    qt = qt_ref[0]
    feat = lax.broadcasted_iota(jnp.int32, qt.shape, 0)
    zero = jnp.zeros_like(qt)
    half = LANES // 2
    qs_ref[:, 0:tq] = jnp.where(feat < half, qt, zero)
    qs_ref[:, tq:2 * tq] = jnp.where(feat >= half, qt, zero)
    m_ref[...] = jnp.full(m_ref.shape, NEG_BIG, F32)
    l_ref[...] = jnp.zeros(l_ref.shape, F32)
    acc_ref[...] = jnp.zeros(acc_ref.shape, F32)

    def step(ki, masked):
        start = pl.multiple_of(ki * tq, tq)
        s = jnp.dot(k_ref[0, pl.ds(start, tq), :], qs_ref[...], preferred_element_type=F32)
        if masked:
            kpos = lax.broadcasted_iota(jnp.int32, s.shape, 0)
            qpos = lax.broadcasted_iota(jnp.int32, s.shape, 1) % tq
            s = jnp.where(kpos <= qpos, s, NEG_BIG)
        m_old = m_ref[...]
        m_new = jnp.maximum(m_old, jnp.max(s, axis=0, keepdims=True))
        alpha = jnp.exp(m_old - m_new)
        p = jnp.exp(s - m_new)
        l_ref[...] = alpha * l_ref[...] + jnp.sum(p, axis=0, keepdims=True)
        pv = jnp.dot(vt_ref[0, :, pl.ds(start, tq)], p.astype(BF16), preferred_element_type=F32)
        acc_ref[...] = alpha * acc_ref[...] + pv
        m_ref[...] = m_new

    def body(ki, carry):
        step(ki, False)
        return carry

    lax.fori_loop(0, qi, body, 0)
    step(qi, True)

    lam = _diff_lambda(lq1_ref, lk1_ref, lq2_ref, lk2_ref, lam_init)
    n = acc_ref[...] / l_ref[...]
    o = n[:, 0:tq] - lam * n[:, tq:2 * tq]
    ms = jnp.mean(o * o, axis=0, keepdims=True)
    y = ((o * lax.rsqrt(ms + EPS)) * subln_ref[...]) * (1.0 - lam_init)
    o_ref[0] = y.T.astype(BF16)


def _attn_prompt(qt, k, vt, lq1, lk1, lq2, lk2, subln, lam_init, *, tq):
    b, hw, t = qt.shape
    nh = hw // LANES
    assert t % tq == 0
    vec = lambda a: a.reshape(1, -1)
    small = lambda w: pl.BlockSpec((1, w), lambda i, h, j: (0, 0))
    kern = functools.partial(_attn_prompt_kernel, tq=tq, lam_init=lam_init)
    return pl.pallas_call(
        kern, grid=(b, nh, t // tq),
        in_specs=[small(lq1.size), small(lk1.size), small(lq2.size), small(lk2.size),
                  pl.BlockSpec((LANES, 1), lambda i, h, j: (0, 0)),
                  pl.BlockSpec((1, LANES, tq), lambda i, h, j: (i, h, j)),
                  pl.BlockSpec((1, t, LANES), lambda i, h, j: (i, 0, h)),
                  pl.BlockSpec((1, LANES, t), lambda i, h, j: (i, h, 0))],
        out_specs=pl.BlockSpec((1, tq, LANES), lambda i, h, j: (i, j, h)),
        out_shape=jax.ShapeDtypeStruct((b, t, hw), BF16),
        scratch_shapes=[pltpu.VMEM((LANES, 2 * tq), BF16), pltpu.VMEM((1, 2 * tq), F32),
                        pltpu.VMEM((1, 2 * tq), F32), pltpu.VMEM((LANES, 2 * tq), F32)],
        compiler_params=_cparams(("parallel", "parallel", "arbitrary")), name="diff_attn_prompt",
    )(vec(lq1), vec(lk1), vec(lq2), vec(lk2), subln.reshape(LANES, 1), qt, k, vt)


DEC_ROWS = 2 * SUBLANES


def _attn_decode_kernel(pt_ref, lq1_ref, lk1_ref, lq2_ref, lk2_ref, subln_ref, q_ref, kn_ref, vn_ref, *rest,
                        pages, n_heads, t_new, lam_init):
    k_refs = rest[:pages]
    v_refs = rest[pages:2 * pages]
    o_ref = rest[2 * pages]
    m_ref, l_ref, acc_ref = rest[2 * pages + 1:]
    j = pl.program_id(1)

    @pl.when(j == 0)
    def _():
        m_ref[...] = jnp.full(m_ref.shape, NEG_BIG, F32)
        l_ref[...] = jnp.zeros(l_ref.shape, F32)
        acc_ref[...] = jnp.zeros(acc_ref.shape, F32)

    page = k_refs[0].shape[2]
    for h in range(n_heads):
        sl = slice(h * LANES, (h + 1) * LANES)
        kt = jnp.concatenate([kr[0, sl, :].astype(BF16) for kr in k_refs], axis=1)
        vt = jnp.concatenate([vr[0, pl.ds(h, page, stride=n_heads), :].astype(BF16) for vr in v_refs], axis=0)
        s = jnp.dot(q_ref[0, h], kt, preferred_element_type=F32)
        _online_softmax_step(s, vt, m_ref.at[h], l_ref.at[h], acc_ref.at[h])

    @pl.when(j == pl.num_programs(1) - 1)
    def _():
        lam = _diff_lambda(lq1_ref, lk1_ref, lq2_ref, lk2_ref, lam_init)
        n_new = kn_ref.shape[1]
        qt = lax.broadcasted_iota(jnp.int32, (DEC_ROWS, n_new), 0) % SUBLANES
        kt_pos = lax.broadcasted_iota(jnp.int32, (DEC_ROWS, n_new), 1)
        visible = (kt_pos <= qt) & (kt_pos < t_new)
        for h in range(n_heads):
            sl = slice(h * LANES, (h + 1) * LANES)
            s = lax.dot_general(q_ref[0, h], kn_ref[0, :, sl], (((1,), (1,)), ((), ())),
                                preferred_element_type=F32)
            s = jnp.where(visible, s, NEG_BIG)
            _online_softmax_step(s, vn_ref[0, :, sl], m_ref.at[h], l_ref.at[h], acc_ref.at[h])
            n = acc_ref[h] / l_ref[h]
            o_ref[0, :, sl] = _head_out(n[0:SUBLANES], n[SUBLANES:DEC_ROWS], lam, subln_ref[...], 1.0 - lam_init)


def _attn_decode(qd, k_new, v_new, cache_k, cache_v, page_table, lq1, lk1, lq2, lk2, subln, lam_init, *,
                 pages, t_new):
    b, nh = qd.shape[0], qd.shape[1]
    hw = nh * LANES
    n_pages = page_table.shape[1]
    assert n_pages % pages == 0
    vec = lambda a: a.reshape(1, -1)
    small = lambda w: pl.BlockSpec((1, w), lambda i, j, pt: (0, 0))

    def page_spec(cache, p):
        return pl.BlockSpec((1,) + cache.shape[1:], lambda i, j, pt: (pt[i, j * pages + p], 0, 0))

    kern = functools.partial(_attn_decode_kernel, pages=pages, n_heads=nh, t_new=t_new, lam_init=lam_init)
    grid_spec = pltpu.PrefetchScalarGridSpec(
        num_scalar_prefetch=1, grid=(b, n_pages // pages),
        in_specs=[small(lq1.size), small(lk1.size), small(lq2.size), small(lk2.size), small(LANES),
                  pl.BlockSpec((1, nh, DEC_ROWS, LANES), lambda i, j, pt: (i, 0, 0, 0)),
                  pl.BlockSpec((1, k_new.shape[1], hw), lambda i, j, pt: (i, 0, 0)),
                  pl.BlockSpec((1, v_new.shape[1], hw), lambda i, j, pt: (i, 0, 0))]
                 + [page_spec(cache_k, p) for p in range(pages)] + [page_spec(cache_v, p) for p in range(pages)],
        out_specs=pl.BlockSpec((1, SUBLANES, hw), lambda i, j, pt: (i, 0, 0)),
        scratch_shapes=[pltpu.VMEM((nh, DEC_ROWS, 1), F32), pltpu.VMEM((nh, DEC_ROWS, 1), F32),
                        pltpu.VMEM((nh, DEC_ROWS, LANES), F32)])
    return pl.pallas_call(
        kern, grid_spec=grid_spec, out_shape=jax.ShapeDtypeStruct((b, SUBLANES, hw), F32),
        compiler_params=_cparams(("parallel", "arbitrary")), name="diff_attn_decode",
    )(page_table, vec(lq1), vec(lk1), vec(lq2), vec(lk2), vec(subln), qd, k_new, v_new,
      *([cache_k] * pages), *([cache_v] * pages))


def _rope_tables(pos, head_dim, batch):
    half = head_dim // 2
    inv = ROPE_THETA ** (-jnp.arange(half, dtype=F32) / half)
    ang = pos.astype(F32)[:, None] * inv[None, :]
    cos, sin = jnp.cos(ang), jnp.sin(ang)
    reps = LANES // head_dim
    cos_t = jnp.tile(jnp.concatenate([cos, cos], axis=1), (batch, reps))
    sin_t = jnp.tile(jnp.concatenate([-sin, sin], axis=1), (batch, reps))
    return cos_t, sin_t


def _trunk(x, pos, conv0, h0, decode, p, cfg):
    b, t, d = x.shape
    m = b * t
    depth = p['norm_mix'].shape[0]
    n_a = p['rg_w_x'].shape[0]
    head_dim = p['dif_lq1'].shape[1]
    v_dim = p['dif_subln'].shape[1]
    qk_width = p['dif_w_q'].shape[2]
    n_heads = qk_width // (2 * head_dim)
    tm, tn = cfg['tm'], cfg['tn']
    x = x.reshape(m, d)
    cos_t, sin_t = _rope_tables(pos, head_dim, b)
    new_bufs, new_hs = [], []
    k_out = v_new = kb = vb = None

    t_pad = -(-t // SUBLANES) * SUBLANES

    prompt = decode is None
    plain = lambda *dts: [(dt, False) for dt in dts]
    for l in range(depth):
        if l < n_a:
            gate, u = _norm_matmul(x, p['norm_mix'][l], [p['rg_w_gate'][l], p['rg_w_x'][l]], [], plain(F32, F32),
                                   _epi_recurrent_in, tm=tm, tn=tn, name="recurrent_in")
            width = u.shape[1]
            u3, g3 = u.reshape(b, t, width), gate.reshape(b, t, width)
            if t_pad != t:
                u3 = jnp.pad(u3, ((0, 0), (0, t_pad - t), (0, 0)))
                g3 = jnp.pad(g3, ((0, 0), (0, t_pad - t), (0, 0)))
            hg, nb, nh = _conv_rglru(u3, g3, conv0[l], h0[l], p['rg_conv_w'][l], p['rg_conv_b'][l],
                                     p['rg_w_a'][l], p['rg_b_a'][l], p['rg_w_i'][l], p['rg_b_i'][l],
                                     p['rg_lambda'][l], tc=cfg['tc'], t_valid=t)
            new_bufs.append(nb)
            new_hs.append(nh.reshape(b, width))
            hg = hg[:, :t].reshape(m, width)
            x = _matmul_residual(hg, p['rg_w_out'][l], x, tm=tm, tn=tn, name="recurrent_out")
        else:
            j = l - n_a
            if j == 0:
                w_kv = p['w_kv']
                kv_ws = [w_kv[:, :qk_width], w_kv[:, qk_width:]]
                if prompt:
                    k_t, kb, v_new, vb = _norm_matmul(
                        x, p['kv_norm'], kv_ws, [cos_t, sin_t],
                        [(F32, True), (BF16, False), (F32, False), (BF16, True)], _epi_kv_transposed,
                        tm=tm, tn=tn, name="shared_kv", seq_len=t)
                    k_out = k_t.reshape(b, n_heads, 2, head_dim, t).transpose(0, 4, 1, 2, 3)
                else:
                    k_new, kb, v_new, vb = _norm_matmul(
                        x, p['kv_norm'], kv_ws, [cos_t, sin_t], plain(F32, BF16, F32, BF16), _epi_kv,
                        tm=tm, tn=tn, name="shared_kv")
                    k_out = k_new.reshape(b, t, n_heads, 2, head_dim)
            lam_init = 0.8 - 0.6 * math.exp(-0.3 * l)
            (q,) = _norm_matmul(x, p['norm_mix'][l], [p['dif_w_q'][j]], [cos_t, sin_t], [(BF16, prompt)],
                                functools.partial(_epi_q, scale=head_dim ** -0.5, transposed=prompt),
                                tm=tm, tn=tn, name="q_proj", seq_len=t)
            lams = (p['dif_lq1'][j], p['dif_lk1'][j], p['dif_lq2'][j], p['dif_lk2'][j], p['dif_subln'][j])
            if prompt:
                o = _attn_prompt(q, kb.reshape(b, t, qk_width), vb, *lams, lam_init, tq=cfg['tq'])
                o = o.reshape(m, n_heads * v_dim)
            else:
                cache_k, cache_v, page_table = decode
                q4 = q.reshape(b, t, n_heads, LANES).transpose(0, 2, 1, 3)
                lane = jnp.arange(LANES)
                zpad = jnp.zeros((b, n_heads, SUBLANES - t, LANES), BF16)
                qd = jnp.concatenate([jnp.where(lane < head_dim, q4, 0).astype(BF16), zpad,
                                      jnp.where(lane >= head_dim, q4, 0).astype(BF16), zpad], axis=2)
                pad_new = lambda a: jnp.pad(a.reshape(b, t, -1), ((0, 0), (0, DEC_ROWS - t), (0, 0)))
                o = _attn_decode(qd, pad_new(kb), pad_new(vb), cache_k, cache_v, page_table, *lams, lam_init,
                                 pages=cfg['pages'], t_new=t)
                o = o[:, :t].reshape(m, n_heads * v_dim).astype(BF16)
            x = _matmul_residual(o, p['dif_w_o'][j], x, tm=tm, tn=tn, name="attn_out")
        (hmid,) = _norm_matmul(x, p['norm_ffn'][l], [p['ffn_w_gate'][l], p['ffn_w_up'][l]], [], plain(BF16),
                               _epi_swiglu, tm=tm, tn=tn, name="ffn_in")
        x = _matmul_residual(hmid, p['ffn_w_down'][l], x, tm=cfg['tm_down'], tn=tn, name="ffn_down")

    y = _rmsnorm(x, p['norm_final'], tm=cfg['tm_norm']).reshape(b, t, d)
    v_out = v_new.reshape(b, t, n_heads, v_dim)
    return y, k_out, v_out, jnp.stack(new_bufs), jnp.stack(new_hs)


_MATMUL_WEIGHTS = ('rg_w_x', 'rg_w_gate', 'rg_w_a', 'rg_w_i', 'rg_w_out', 'w_kv', 'dif_w_q', 'dif_w_o',
                   'ffn_w_gate', 'ffn_w_up', 'ffn_w_down')


def kernel(x_prompt, x_sample, cache_k, cache_v, page_table, state_conv, state_rglru, norm_mix, norm_ffn, norm_final, rg_w_x, rg_w_gate, rg_conv_w, rg_conv_b, rg_w_a, rg_b_a, rg_w_i, rg_b_i, rg_lambda, rg_w_out, kv_norm, w_kv, dif_w_q, dif_lq1, dif_lk1, dif_lq2, dif_lk2, dif_subln, dif_w_o, ffn_w_gate, ffn_w_up, ffn_w_down):
    p = dict(norm_mix=norm_mix, norm_ffn=norm_ffn, norm_final=norm_final, rg_w_x=rg_w_x,
             rg_w_gate=rg_w_gate, rg_conv_w=rg_conv_w, rg_conv_b=rg_conv_b, rg_w_a=rg_w_a,
             rg_b_a=rg_b_a, rg_w_i=rg_w_i, rg_b_i=rg_b_i, rg_lambda=rg_lambda, rg_w_out=rg_w_out,
             kv_norm=kv_norm, w_kv=w_kv, dif_w_q=dif_w_q, dif_lq1=dif_lq1, dif_lk1=dif_lk1,
             dif_lq2=dif_lq2, dif_lk2=dif_lk2, dif_subln=dif_subln, dif_w_o=dif_w_o,
             ffn_w_gate=ffn_w_gate, ffn_w_up=ffn_w_up, ffn_w_down=ffn_w_down)
    for name in _MATMUL_WEIGHTS:
        p[name] = p[name].astype(BF16)

    n_a = rg_w_x.shape[0]
    width = rg_w_x.shape[2]
    b_p, t_p = x_prompt.shape[0], x_prompt.shape[1]
    cfg_p = dict(tm=1024, tn=512, tm_down=512, tm_norm=512, tc=256, tq=512)
    conv0 = jnp.zeros((n_a, b_p, CONV_WIDTH - 1, width), F32)
    h0 = jnp.zeros((n_a, b_p, width), F32)
    y_p, k_p, v_p, conv_p, h_p = _trunk(x_prompt, jnp.arange(t_p, dtype=jnp.int32), conv0, h0, None, p, cfg_p)

    b_s, t_s = x_sample.shape[0], x_sample.shape[1]
    n_phys, page = cache_k.shape[0], cache_k.shape[1]
    past_len = page_table.shape[1] * page
    cfg_s = dict(tm=b_s * t_s, tn=512, tm_down=b_s * t_s, tm_norm=b_s * t_s, tc=SUBLANES, pages=8)
    k_pages = jnp.transpose(cache_k, (0, 2, 3, 4, 1)).reshape(n_phys, -1, page)
    v_pages = cache_v.reshape(n_phys, page * cache_v.shape[2], cache_v.shape[3])
    decode = (k_pages, v_pages, page_table)
    pos_s = past_len + jnp.arange(t_s, dtype=jnp.int32)
    y_s, k_s, v_s, conv_s, h_s = _trunk(x_sample, pos_s, state_conv, state_rglru, decode, p, cfg_s)
    return (y_p, y_s, k_p, v_p, conv_p, h_p, k_s, v_s, conv_s, h_s)
```

```python
import functools
import math

import jax
import jax.numpy as jnp
from jax import lax
from jax.experimental import pallas as pl
from jax.experimental.pallas import tpu as pltpu

F32 = jnp.float32
BF16 = jnp.bfloat16

EPS = 1e-6
RG_C = 8.0
ROPE_THETA = 10000.0
CONV_WIDTH = 4
LANES = 128
SUBLANES = 8
NEG_BIG = -1e30
VMEM_LIMIT = 56 * 1024 * 1024


def _cparams(sem):
    return pltpu.CompilerParams(dimension_semantics=sem, vmem_limit_bytes=VMEM_LIMIT)


def _nmm_kernel(*refs, n_w, n_extra, n_out, epi):
    x_ref, g_ref = refs[0], refs[1]
    w_refs = refs[2:2 + n_w]
    e_refs = refs[2 + n_w:2 + n_w + n_extra]
    o_refs = refs[2 + n_w + n_extra:2 + n_w + n_extra + n_out]
    xn_ref = refs[-1]

    @pl.when(pl.program_id(1) == 0)
    def _():
        x = x_ref[...]
        ms = jnp.mean(x * x, axis=-1, keepdims=True)
        xn_ref[...] = ((x * lax.rsqrt(ms + EPS)) * g_ref[...]).astype(BF16)

    xn = xn_ref[...]
    accs = [jnp.dot(xn, w[...], preferred_element_type=F32) for w in w_refs]
    epi(accs, e_refs, o_refs)


def _norm_matmul(x, g, ws, extras, outs, epi, *, tm, tn, name, seq_len=None, n=None, col_starts=None):
    m, d = x.shape
    n = ws[0].shape[1] if n is None else n
    col_starts = [0] * len(ws) if col_starts is None else col_starts
    tm = min(tm, m)
    grid = (m // tm, n // tn)
    in_specs = [pl.BlockSpec((tm, d), lambda i, j: (i, 0)),
                pl.BlockSpec((1, d), lambda i, j: (0, 0))]
    in_specs += [pl.BlockSpec((d, tn), functools.partial(lambda i, j, first: (0, first + j), first=c // tn))
                 for c in col_starts]
    in_specs += [pl.BlockSpec((tm, LANES), lambda i, j: (i, 0)) for _ in extras]
    out_specs, out_shape = [], []
    for dt, transposed in outs:
        if transposed:
            per_seq = seq_len // tm
            out_specs.append(pl.BlockSpec((1, tn, tm), lambda i, j: (i // per_seq, j, i % per_seq)))
            out_shape.append(jax.ShapeDtypeStruct((m // seq_len, n, seq_len), dt))
        else:
            out_specs.append(pl.BlockSpec((tm, tn), lambda i, j: (i, j)))
            out_shape.append(jax.ShapeDtypeStruct((m, n), dt))
    kern = functools.partial(_nmm_kernel, n_w=len(ws), n_extra=len(extras), n_out=len(outs), epi=epi)
    return pl.pallas_call(
        kern, grid=grid, in_specs=in_specs, out_specs=out_specs, out_shape=out_shape,
        scratch_shapes=[pltpu.VMEM((tm, d), BF16)],
        compiler_params=_cparams(("parallel", "arbitrary")), name=name,
    )(x, g.reshape(1, d), *ws, *extras)


def _epi_recurrent_in(accs, e_refs, o_refs):
    o_refs[0][...] = jax.nn.gelu(accs[0])
    o_refs[1][...] = accs[1]


def _epi_swiglu(accs, e_refs, o_refs):
    o_refs[0][...] = (jax.nn.silu(accs[0]) * accs[1]).astype(BF16)


def _rope_slab(y, cos, sin_signed):
    lane = lax.broadcasted_iota(jnp.int32, y.shape, 1)
    first_half = (lane % 64) < 32
    rot = jnp.where(first_half, pltpu.roll(y, LANES - 32, 1), pltpu.roll(y, 32, 1))
    return y * cos + rot * sin_signed


def _epi_q(accs, e_refs, o_refs, *, scale, transposed):
    cos, sin_signed = e_refs[0][...], e_refs[1][...]
    for s in range(accs[0].shape[1] // LANES):
        sl = slice(s * LANES, (s + 1) * LANES)
        q = _rope_slab(accs[0][:, sl], cos, sin_signed) * scale
        if transposed:
            o_refs[0][0, sl, :] = q.T.astype(BF16)
        else:
            o_refs[0][:, sl] = q.astype(BF16)


def _epi_kv(accs, e_refs, o_refs):
    cos, sin_signed = e_refs[0][...], e_refs[1][...]
    for s in range(accs[0].shape[1] // LANES):
        sl = slice(s * LANES, (s + 1) * LANES)
        k = _rope_slab(accs[0][:, sl], cos, sin_signed)
        o_refs[0][:, sl] = k
        o_refs[1][:, sl] = k.astype(BF16)
    o_refs[2][...] = accs[1]
    o_refs[3][...] = accs[1].astype(BF16)


def _epi_kv_transposed(accs, e_refs, o_refs):
    cos, sin_signed = e_refs[0][...], e_refs[1][...]
    for s in range(accs[0].shape[1] // LANES):
        sl = slice(s * LANES, (s + 1) * LANES)
        k = _rope_slab(accs[0][:, sl], cos, sin_signed)
        o_refs[0][0, sl, :] = k.T
        o_refs[1][:, sl] = k.astype(BF16)
        o_refs[3][0, sl, :] = accs[1][:, sl].T.astype(BF16)
    o_refs[2][...] = accs[1]


def _mm_res_kernel(a_ref, w_ref, r_ref, o_ref):
    o_ref[...] = r_ref[...] + jnp.dot(a_ref[...], w_ref[...], preferred_element_type=F32)


def _matmul_residual(a, w, res, *, tm, tn, name):
    m, k = a.shape
    n = w.shape[1]
    tm = min(tm, m)
    return pl.pallas_call(
        _mm_res_kernel, grid=(m // tm, n // tn),
        in_specs=[pl.BlockSpec((tm, k), lambda i, j: (i, 0)),
                  pl.BlockSpec((k, tn), lambda i, j: (0, j)),
                  pl.BlockSpec((tm, tn), lambda i, j: (i, j))],
        out_specs=pl.BlockSpec((tm, tn), lambda i, j: (i, j)),
        out_shape=jax.ShapeDtypeStruct((m, n), F32),
        compiler_params=_cparams(("parallel", "arbitrary")), name=name,
    )(a, w, res)


def _rmsnorm_kernel(x_ref, g_ref, o_ref):
    x = x_ref[...]
    ms = jnp.mean(x * x, axis=-1, keepdims=True)
    o_ref[...] = (x * lax.rsqrt(ms + EPS)) * g_ref[...]


def _rmsnorm(x, g, *, tm):
    m, d = x.shape
    tm = min(tm, m)
    return pl.pallas_call(
        _rmsnorm_kernel, grid=(m // tm,),
        in_specs=[pl.BlockSpec((tm, d), lambda i: (i, 0)), pl.BlockSpec((1, d), lambda i: (0, 0))],
        out_specs=pl.BlockSpec((tm, d), lambda i: (i, 0)),
        out_shape=jax.ShapeDtypeStruct((m, d), F32),
        compiler_params=_cparams(("parallel",)), name="final_rmsnorm",
    )(x, g.reshape(1, d))


_EXPM1_SERIES_BOUND = 2.0 ** -6
_EXPM1_SERIES_TERMS = 4


def _neg_expm1(x):
    poly = jnp.full_like(x, 1.0 / math.factorial(_EXPM1_SERIES_TERMS))
    for k in range(_EXPM1_SERIES_TERMS - 1, 0, -1):
        poly = poly * x + 1.0 / math.factorial(k)
    return jnp.where(x > -_EXPM1_SERIES_BOUND, -(poly * x), 1.0 - jnp.exp(x))


def _scan_kernel(u_ref, gate_ref, cw_ref, cb_ref, wa_ref, ba_ref, wi_ref, bi_ref, lam_ref, conv0_ref, h0_ref,
                 hg_ref, convout_ref, hlast_ref, ubuf, a_s, b_s, hs, hcar, *, tc, last_row, n_gate_blocks):
    c = pl.program_id(1)
    width = u_ref.shape[2]
    pad = SUBLANES
    hist = CONV_WIDTH - 1

    @pl.when(c == 0)
    def _():
        ubuf[0:pad, :] = jnp.zeros((pad, width), F32)
        ubuf[pad - hist:pad, :] = conv0_ref[0]
        hcar[...] = jnp.broadcast_to(h0_ref[0], (SUBLANES, width))

    u = u_ref[0]
    ubuf[pad:pad + tc, :] = u
    cw = cw_ref[...]
    xc = cb_ref[...] + u * cw[0:1, :]
    for j in range(1, CONV_WIDTH):
        xc = xc + ubuf[pad - j:pad - j + tc, :] * cw[j:j + 1, :]

    neg_c_softplus = -RG_C * jax.nn.softplus(-lam_ref[...])
    gb = width // n_gate_blocks
    for n in range(n_gate_blocks):
        sl = slice(n * gb, (n + 1) * gb)
        xs = xc[:, sl]
        xb = xs.astype(BF16)
        r = jax.nn.sigmoid(jnp.dot(xb, wa_ref[n], preferred_element_type=F32) + ba_ref[:, sl])
        i = jax.nn.sigmoid(jnp.dot(xb, wi_ref[n], preferred_element_type=F32) + bi_ref[:, sl])
        log_a = r * neg_c_softplus[:, sl]
        a_s[:, sl] = jnp.exp(log_a)
        b_s[:, sl] = jnp.sqrt(_neg_expm1(2.0 * log_a)) * i * xs

    row = lax.broadcasted_iota(jnp.int32, (SUBLANES, width), 0)

    def group(g, h):
        base = pl.multiple_of(g * SUBLANES, SUBLANES)
        a8 = a_s[pl.ds(base, SUBLANES), :]
        b8 = b_s[pl.ds(base, SUBLANES), :]
        out = jnp.zeros((SUBLANES, width), F32)
        for r in range(SUBLANES):
            hn = a8 * h + b8
            out = jnp.where(row == r, hn, out)
            h = jnp.broadcast_to(hn[r:r + 1, :], (SUBLANES, width))
        hs[pl.ds(base, SUBLANES), :] = out
        return h

    hcar[...] = lax.fori_loop(0, tc // SUBLANES, group, hcar[...])
    hg_ref[0] = (hs[...] * gate_ref[0]).astype(BF16)
    ubuf[0:pad, :] = ubuf[tc:tc + pad, :]

    @pl.when(c == pl.num_programs(1) - 1)
    def _():
        convout_ref[0] = ubuf[pad + last_row + 1 - hist:pad + last_row + 1, :]
        hlast_ref[0] = hs[last_row:last_row + 1, :]


def _conv_rglru(u, gate, conv0, h0, cw, cb, wa, ba, wi, bi, lam, *, tc, t_valid):
    b, t, width = u.shape
    tc = min(tc, t)
    n_chunks = t // tc
    last_row = (t_valid - 1) - (n_chunks - 1) * tc
    nb = wa.shape[0]
    row2 = lambda v: v.reshape(1, width)
    kern = functools.partial(_scan_kernel, tc=tc, last_row=last_row, n_gate_blocks=nb)
    full2 = lambda shape: pl.BlockSpec(shape, lambda i, c: (0, 0))
    return pl.pallas_call(
        kern, grid=(b, n_chunks),
        in_specs=[pl.BlockSpec((1, tc, width), lambda i, c: (i, c, 0)),
                  pl.BlockSpec((1, tc, width), lambda i, c: (i, c, 0)),
                  full2((CONV_WIDTH, width)), full2((1, width)),
                  pl.BlockSpec(wa.shape, lambda i, c: (0, 0, 0)), full2((1, width)),
                  pl.BlockSpec(wi.shape, lambda i, c: (0, 0, 0)), full2((1, width)),
                  full2((1, width)),
                  pl.BlockSpec((1, CONV_WIDTH - 1, width), lambda i, c: (i, 0, 0)),
                  pl.BlockSpec((1, 1, width), lambda i, c: (i, 0, 0))],
        out_specs=[pl.BlockSpec((1, tc, width), lambda i, c: (i, c, 0)),
                   pl.BlockSpec((1, CONV_WIDTH - 1, width), lambda i, c: (i, 0, 0)),
                   pl.BlockSpec((1, 1, width), lambda i, c: (i, 0, 0))],
        out_shape=[jax.ShapeDtypeStruct((b, t, width), BF16),
                   jax.ShapeDtypeStruct((b, CONV_WIDTH - 1, width), F32),
                   jax.ShapeDtypeStruct((b, 1, width), F32)],
        scratch_shapes=[pltpu.VMEM((tc + SUBLANES, width), F32),
                        pltpu.VMEM((tc, width), F32), pltpu.VMEM((tc, width), F32),
                        pltpu.VMEM((tc, width), F32), pltpu.VMEM((SUBLANES, width), F32)],
        compiler_params=_cparams(("parallel", "arbitrary")), name="conv_rglru",
    )(u, gate, cw, row2(cb), wa, row2(ba), wi, row2(bi), row2(lam), conv0, h0.reshape(b, 1, width))


def _diff_lambda(lq1_ref, lk1_ref, lq2_ref, lk2_ref, lam_init):
    e1 = jnp.exp(jnp.sum(lq1_ref[...] * lk1_ref[...], axis=-1, keepdims=True))
    e2 = jnp.exp(jnp.sum(lq2_ref[...] * lk2_ref[...], axis=-1, keepdims=True))
    return e1 - e2 + lam_init


LOG2_E = math.log2(math.e)


def _online_softmax_step(s, v, m_ref, l_ref, acc_ref):
    m_old = m_ref[...]
    m_new = jnp.maximum(m_old, jnp.max(s, axis=-1, keepdims=True))
    alpha = jnp.exp2(m_old - m_new)
    p = jnp.exp2(s - m_new)
    l_ref[...] = alpha * l_ref[...] + jnp.sum(p, axis=-1, keepdims=True)
    acc_ref[...] = alpha * acc_ref[...] + jnp.dot(p.astype(BF16), v, preferred_element_type=F32)
    m_ref[...] = m_new


def _head_out(n0, n1, lam, subln, out_scale):
    o = n0 - lam * n1
    ms = jnp.mean(o * o, axis=-1, keepdims=True)
    return ((o * lax.rsqrt(ms + EPS)) * subln) * out_scale


def _attn_prompt_kernel(lq1_ref, lk1_ref, lq2_ref, lk2_ref, subln_ref, qt_ref, k_ref, vt_ref, o_ref,
                        qs_ref, s_ref, cmax_ref, m_ref, l_ref, acc_ref, *, tq, lam_init):
    qi = pl.program_id(2)
    qt = qt_ref[0]
    feat = lax.broadcasted_iota(jnp.int32, qt.shape, 0)
    zero = jnp.zeros_like(qt)
    half = LANES // 2
    qs_ref[:, 0:tq] = jnp.where(feat < half, qt, zero)
    qs_ref[:, tq:2 * tq] = jnp.where(feat >= half, qt, zero)
    m_ref[...] = jnp.full(m_ref.shape, NEG_BIG, F32)
    l_ref[...] = jnp.zeros(l_ref.shape, F32)
    acc_ref[...] = jnp.zeros(acc_ref.shape, F32)

    def produce(ki, slot):
        start = pl.multiple_of(ki * tq, tq)
        s = jnp.dot(k_ref[0, pl.ds(start, tq), :], qs_ref[...], preferred_element_type=F32)
        s_ref[slot] = s
        cmax_ref[slot] = jnp.max(s, axis=0, keepdims=True)

    def consume(ki, slot, masked):
        start = pl.multiple_of(ki * tq, tq)
        s = s_ref[slot]
        cmax = cmax_ref[slot]
        if masked:
            kpos = lax.broadcasted_iota(jnp.int32, s.shape, 0)
            qpos = lax.broadcasted_iota(jnp.int32, s.shape, 1) % tq
            s = jnp.where(kpos <= qpos, s, NEG_BIG)
            cmax = jnp.max(s, axis=0, keepdims=True)
        m_old = m_ref[...]
        m_new = jnp.maximum(m_old, cmax)
        alpha = jnp.exp2(m_old - m_new)
        p = jnp.exp2(s - m_new)
        l_ref[...] = alpha * l_ref[...] + jnp.sum(p, axis=0, keepdims=True)
        pv = jnp.dot(vt_ref[0, :, pl.ds(start, tq)], p.astype(BF16), preferred_element_type=F32)
        acc_ref[...] = alpha * acc_ref[...] + pv
        m_ref[...] = m_new

    produce(0, 0)

    def pair(i, carry):
        produce(2 * i + 1, 1)
        consume(2 * i, 0, False)
        produce(2 * i + 2, 0)
        consume(2 * i + 1, 1, False)
        return carry

    lax.fori_loop(0, qi // 2, pair, 0)

    @pl.when(qi % 2 == 0)
    def _():
        consume(qi, 0, True)

    @pl.when(qi % 2 == 1)
    def _():
        produce(qi, 1)
        consume(qi - 1, 0, False)
        consume(qi, 1, True)

    lam = _diff_lambda(lq1_ref, lk1_ref, lq2_ref, lk2_ref, lam_init)
    n = acc_ref[...] / l_ref[...]
    o = n[:, 0:tq] - lam * n[:, tq:2 * tq]
    ms = jnp.mean(o * o, axis=0, keepdims=True)
    y = ((o * lax.rsqrt(ms + EPS)) * subln_ref[...]) * (1.0 - lam_init)
    o_ref[0] = y.T.astype(BF16)


def _attn_prompt(qt, k, vt, lq1, lk1, lq2, lk2, subln, lam_init, *, tq):
    b, hw, t = qt.shape
    nh = hw // LANES
    assert t % tq == 0
    vec = lambda a: a.reshape(1, -1)
    small = lambda w: pl.BlockSpec((1, w), lambda i, h, j: (0, 0))
    kern = functools.partial(_attn_prompt_kernel, tq=tq, lam_init=lam_init)
    return pl.pallas_call(
        kern, grid=(b, nh, t // tq),
        in_specs=[small(lq1.size), small(lk1.size), small(lq2.size), small(lk2.size),
                  pl.BlockSpec((LANES, 1), lambda i, h, j: (0, 0)),
                  pl.BlockSpec((1, LANES, tq), lambda i, h, j: (i, h, j)),
                  pl.BlockSpec((1, t, LANES), lambda i, h, j: (i, 0, h)),
                  pl.BlockSpec((1, LANES, t), lambda i, h, j: (i, h, 0))],
        out_specs=pl.BlockSpec((1, tq, LANES), lambda i, h, j: (i, j, h)),
        out_shape=jax.ShapeDtypeStruct((b, t, hw), BF16),
        scratch_shapes=[pltpu.VMEM((LANES, 2 * tq), BF16), pltpu.VMEM((2, tq, 2 * tq), F32),
                        pltpu.VMEM((2, 1, 2 * tq), F32), pltpu.VMEM((1, 2 * tq), F32),
                        pltpu.VMEM((1, 2 * tq), F32), pltpu.VMEM((LANES, 2 * tq), F32)],
        compiler_params=_cparams(("parallel", "parallel", "arbitrary")), name="diff_attn_prompt",
    )(vec(lq1), vec(lk1), vec(lq2), vec(lk2), subln.reshape(LANES, 1), qt, k, vt)


DEC_ROWS = 2 * SUBLANES


def _attn_decode_kernel(pt_ref, lq1_ref, lk1_ref, lq2_ref, lk2_ref, subln_ref, q_ref, kn_ref, vn_ref, *rest,
                        pages, n_heads, t_new, lam_init):
    k_refs = rest[:pages]
    v_refs = rest[pages:2 * pages]
    o_ref = rest[2 * pages]
    m_ref, l_ref, acc_ref, vh_ref = rest[2 * pages + 1:]
    j = pl.program_id(1)

    @pl.when(j == 0)
    def _():
        m_ref[...] = jnp.full(m_ref.shape, NEG_BIG, F32)
        l_ref[...] = jnp.zeros(l_ref.shape, F32)
        acc_ref[...] = jnp.zeros(acc_ref.shape, F32)

    page = k_refs[0].shape[2]
    for p, vr in enumerate(v_refs):
        v_heads = jnp.swapaxes(vr[0].reshape(page, n_heads, LANES), 0, 1)
        vh_ref[:, p * page:(p + 1) * page, :] = v_heads.astype(BF16)
    head_lanes = lambda h: slice(h * LANES, (h + 1) * LANES)
    head_rows = lambda h: slice(h * DEC_ROWS, (h + 1) * DEC_ROWS)

    def softmax_step(s_heads, v_of_head):
        s = jnp.concatenate(s_heads, axis=0)
        m_old = m_ref[...]
        m_new = jnp.maximum(m_old, jnp.max(s, axis=-1, keepdims=True))
        alpha = jnp.exp2(m_old - m_new)
        p = jnp.exp2(s - m_new)
        l_ref[...] = alpha * l_ref[...] + jnp.sum(p, axis=-1, keepdims=True)
        pb = p.astype(BF16)
        pv = [jnp.dot(pb[head_rows(h)], v_of_head(h), preferred_element_type=F32) for h in range(n_heads)]
        acc_ref[...] = alpha * acc_ref[...] + jnp.concatenate(pv, axis=0)
        m_ref[...] = m_new

    s_heads = [jnp.dot(q_ref[0, h],
                       jnp.concatenate([kr[0, head_lanes(h), :].astype(BF16) for kr in k_refs], axis=1),
                       preferred_element_type=F32) for h in range(n_heads)]
    softmax_step(s_heads, lambda h: vh_ref[h])

    @pl.when(j == pl.num_programs(1) - 1)
    def _():
        lam = _diff_lambda(lq1_ref, lk1_ref, lq2_ref, lk2_ref, lam_init)
        n_new = kn_ref.shape[1]
        qt = lax.broadcasted_iota(jnp.int32, (DEC_ROWS, n_new), 0) % SUBLANES
        kt_pos = lax.broadcasted_iota(jnp.int32, (DEC_ROWS, n_new), 1)
        visible = (kt_pos <= qt) & (kt_pos < t_new)
        s_new = [jnp.where(visible,
                           lax.dot_general(q_ref[0, h], kn_ref[0, :, head_lanes(h)], (((1,), (1,)), ((), ())),
                                           preferred_element_type=F32), NEG_BIG) for h in range(n_heads)]
        softmax_step(s_new, lambda h: vn_ref[0, :, head_lanes(h)])
        n = acc_ref[...] / l_ref[...]
        for h in range(n_heads):
            nh_ = n[head_rows(h)]
            o_ref[0, :, head_lanes(h)] = _head_out(nh_[0:SUBLANES], nh_[SUBLANES:DEC_ROWS], lam, subln_ref[...],
                                                   1.0 - lam_init)


def _attn_decode(qd, k_new, v_new, cache_k, cache_v, page_table, lq1, lk1, lq2, lk2, subln, lam_init, *,
                 pages, t_new):
    b, nh = qd.shape[0], qd.shape[1]
    hw = nh * LANES
    n_pages = page_table.shape[1]
    assert n_pages % pages == 0
    vec = lambda a: a.reshape(1, -1)
    small = lambda w: pl.BlockSpec((1, w), lambda i, j, pt: (0, 0))

    def page_spec(cache, p):
        return pl.BlockSpec((1,) + cache.shape[1:], lambda i, j, pt: (pt[i, j * pages + p], 0, 0))

    kern = functools.partial(_attn_decode_kernel, pages=pages, n_heads=nh, t_new=t_new, lam_init=lam_init)
    grid_spec = pltpu.PrefetchScalarGridSpec(
        num_scalar_prefetch=1, grid=(b, n_pages // pages),
        in_specs=[small(lq1.size), small(lk1.size), small(lq2.size), small(lk2.size), small(LANES),
                  pl.BlockSpec((1, nh, DEC_ROWS, LANES), lambda i, j, pt: (i, 0, 0, 0)),
                  pl.BlockSpec((1, k_new.shape[1], hw), lambda i, j, pt: (i, 0, 0)),
                  pl.BlockSpec((1, v_new.shape[1], hw), lambda i, j, pt: (i, 0, 0))]
                 + [page_spec(cache_k, p) for p in range(pages)] + [page_spec(cache_v, p) for p in range(pages)],
        out_specs=pl.BlockSpec((1, SUBLANES, hw), lambda i, j, pt: (i, 0, 0)),
        scratch_shapes=[pltpu.VMEM((nh * DEC_ROWS, 1), F32), pltpu.VMEM((nh * DEC_ROWS, 1), F32),
                        pltpu.VMEM((nh * DEC_ROWS, LANES), F32),
                        pltpu.VMEM((nh, pages * cache_k.shape[2], LANES), BF16)])
    return pl.pallas_call(
        kern, grid_spec=grid_spec, out_shape=jax.ShapeDtypeStruct((b, SUBLANES, hw), F32),
        compiler_params=_cparams(("parallel", "arbitrary")), name="diff_attn_decode",
    )(page_table, vec(lq1), vec(lk1), vec(lq2), vec(lk2), vec(subln), qd, k_new, v_new,
      *([cache_k] * pages), *([cache_v] * pages))


def _rope_tables(pos, head_dim, batch):
    half = head_dim // 2
    inv = ROPE_THETA ** (-jnp.arange(half, dtype=F32) / half)
    ang = pos.astype(F32)[:, None] * inv[None, :]
    cos, sin = jnp.cos(ang), jnp.sin(ang)
    reps = LANES // head_dim
    cos_t = jnp.tile(jnp.concatenate([cos, cos], axis=1), (batch, reps))
    sin_t = jnp.tile(jnp.concatenate([-sin, sin], axis=1), (batch, reps))
    return cos_t, sin_t


def _trunk(x, pos, conv0, h0, decode, p, cfg):
    b, t, d = x.shape
    m = b * t
    depth = p['norm_mix'].shape[0]
    n_a = p['rg_w_x'].shape[0]
    head_dim = p['dif_lq1'].shape[1]
    v_dim = p['dif_subln'].shape[1]
    qk_width = p['dif_w_q'].shape[2]
    n_heads = qk_width // (2 * head_dim)
    tm, tn = cfg['tm'], cfg['tn']
    x = x.reshape(m, d)
    cos_t, sin_t = _rope_tables(pos, head_dim, b)
    new_bufs, new_hs = [], []
    k_out = v_new = kb = vb = None

    t_pad = -(-t // SUBLANES) * SUBLANES

    prompt = decode is None
    plain = lambda *dts: [(dt, False) for dt in dts]
    for l in range(depth):
        if l < n_a:
            gate, u = _norm_matmul(x, p['norm_mix'][l], [p['rg_w_gate'][l], p['rg_w_x'][l]], [], plain(F32, F32),
                                   _epi_recurrent_in, tm=tm, tn=tn, name="recurrent_in")
            width = u.shape[1]
            u3, g3 = u.reshape(b, t, width), gate.reshape(b, t, width)
            if t_pad != t:
                u3 = jnp.pad(u3, ((0, 0), (0, t_pad - t), (0, 0)))
                g3 = jnp.pad(g3, ((0, 0), (0, t_pad - t), (0, 0)))
            hg, nb, nh = _conv_rglru(u3, g3, conv0[l], h0[l], p['rg_conv_w'][l], p['rg_conv_b'][l],
                                     p['rg_w_a'][l], p['rg_b_a'][l], p['rg_w_i'][l], p['rg_b_i'][l],
                                     p['rg_lambda'][l], tc=cfg['tc'], t_valid=t)
            new_bufs.append(nb)
            new_hs.append(nh.reshape(b, width))
            hg = hg[:, :t].reshape(m, width)
            x = _matmul_residual(hg, p['rg_w_out'][l], x, tm=tm, tn=tn, name="recurrent_out")
        else:
            j = l - n_a
            if j == 0:
                assert p['w_kv'].shape[1] == 2 * qk_width
                kv_ws = dict(ws=[p['w_kv'], p['w_kv']], n=qk_width, col_starts=[0, qk_width])
                if prompt:
                    k_t, kb, v_new, vb = _norm_matmul(
                        x, p['kv_norm'], extras=[cos_t, sin_t],
                        outs=[(F32, True), (BF16, False), (F32, False), (BF16, True)], epi=_epi_kv_transposed,
                        tm=tm, tn=tn, name="shared_kv", seq_len=t, **kv_ws)
                    k_out = k_t.reshape(b, n_heads, 2, head_dim, t).transpose(0, 4, 1, 2, 3)
                else:
                    k_new, kb, v_new, vb = _norm_matmul(
                        x, p['kv_norm'], extras=[cos_t, sin_t], outs=plain(F32, BF16, F32, BF16), epi=_epi_kv,
                        tm=tm, tn=tn, name="shared_kv", **kv_ws)
                    k_out = k_new.reshape(b, t, n_heads, 2, head_dim)
            lam_init = 0.8 - 0.6 * math.exp(-0.3 * l)
            (q,) = _norm_matmul(x, p['norm_mix'][l], [p['dif_w_q'][j]], [cos_t, sin_t], [(BF16, prompt)],
                                functools.partial(_epi_q, scale=head_dim ** -0.5 * LOG2_E, transposed=prompt),
                                tm=tm, tn=tn, name="q_proj", seq_len=t)
            lams = (p['dif_lq1'][j], p['dif_lk1'][j], p['dif_lq2'][j], p['dif_lk2'][j], p['dif_subln'][j])
            if prompt:
                o = _attn_prompt(q, kb.reshape(b, t, qk_width), vb, *lams, lam_init, tq=cfg['tq'])
                o = o.reshape(m, n_heads * v_dim)
            else:
                cache_k, cache_v, page_table = decode
                q4 = q.reshape(b, t, n_heads, LANES).transpose(0, 2, 1, 3)
                lane = jnp.arange(LANES)
                zpad = jnp.zeros((b, n_heads, SUBLANES - t, LANES), BF16)
                qd = jnp.concatenate([jnp.where(lane < head_dim, q4, 0).astype(BF16), zpad,
                                      jnp.where(lane >= head_dim, q4, 0).astype(BF16), zpad], axis=2)
                pad_new = lambda a: jnp.pad(a.reshape(b, t, -1), ((0, 0), (0, DEC_ROWS - t), (0, 0)))
                o = _attn_decode(qd, pad_new(kb), pad_new(vb), cache_k, cache_v, page_table, *lams, lam_init,
                                 pages=cfg['pages'], t_new=t)
                o = o[:, :t].reshape(m, n_heads * v_dim).astype(BF16)
            x = _matmul_residual(o, p['dif_w_o'][j], x, tm=tm, tn=tn, name="attn_out")
        (hmid,) = _norm_matmul(x, p['norm_ffn'][l], [p['ffn_w_gate'][l], p['ffn_w_up'][l]], [], plain(BF16),
                               _epi_swiglu, tm=tm, tn=tn, name="ffn_in")
        x = _matmul_residual(hmid, p['ffn_w_down'][l], x, tm=cfg['tm_down'], tn=tn, name="ffn_down")

    y = _rmsnorm(x, p['norm_final'], tm=cfg['tm_norm']).reshape(b, t, d)
    v_out = v_new.reshape(b, t, n_heads, v_dim)
    return y, k_out, v_out, jnp.stack(new_bufs), jnp.stack(new_hs)


_MATMUL_WEIGHTS = ('rg_w_x', 'rg_w_gate', 'rg_w_a', 'rg_w_i', 'rg_w_out', 'w_kv', 'dif_w_q', 'dif_w_o',
                   'ffn_w_gate', 'ffn_w_up', 'ffn_w_down')


def kernel(x_prompt, x_sample, cache_k, cache_v, page_table, state_conv, state_rglru, norm_mix, norm_ffn, norm_final, rg_w_x, rg_w_gate, rg_conv_w, rg_conv_b, rg_w_a, rg_b_a, rg_w_i, rg_b_i, rg_lambda, rg_w_out, kv_norm, w_kv, dif_w_q, dif_lq1, dif_lk1, dif_lq2, dif_lk2, dif_subln, dif_w_o, ffn_w_gate, ffn_w_up, ffn_w_down):
    p = dict(norm_mix=norm_mix, norm_ffn=norm_ffn, norm_final=norm_final, rg_w_x=rg_w_x,
             rg_w_gate=rg_w_gate, rg_conv_w=rg_conv_w, rg_conv_b=rg_conv_b, rg_w_a=rg_w_a,
             rg_b_a=rg_b_a, rg_w_i=rg_w_i, rg_b_i=rg_b_i, rg_lambda=rg_lambda, rg_w_out=rg_w_out,
             kv_norm=kv_norm, w_kv=w_kv, dif_w_q=dif_w_q, dif_lq1=dif_lq1, dif_lk1=dif_lk1,
             dif_lq2=dif_lq2, dif_lk2=dif_lk2, dif_subln=dif_subln, dif_w_o=dif_w_o,
             ffn_w_gate=ffn_w_gate, ffn_w_up=ffn_w_up, ffn_w_down=ffn_w_down)
    for name in _MATMUL_WEIGHTS:
        p[name] = p[name].astype(BF16)

    n_a = rg_w_x.shape[0]
    width = rg_w_x.shape[2]
    b_p, t_p = x_prompt.shape[0], x_prompt.shape[1]
    cfg_p = dict(tm=1024, tn=512, tm_down=1024, tm_norm=512, tc=256, tq=512)
    conv0 = jnp.zeros((n_a, b_p, CONV_WIDTH - 1, width), F32)
    h0 = jnp.zeros((n_a, b_p, width), F32)
    y_p, k_p, v_p, conv_p, h_p = _trunk(x_prompt, jnp.arange(t_p, dtype=jnp.int32), conv0, h0, None, p, cfg_p)

    b_s, t_s = x_sample.shape[0], x_sample.shape[1]
    n_phys, page = cache_k.shape[0], cache_k.shape[1]
    past_len = page_table.shape[1] * page
    cfg_s = dict(tm=b_s * t_s, tn=512, tm_down=b_s * t_s, tm_norm=b_s * t_s, tc=SUBLANES, pages=8)
    k_pages = jnp.transpose(cache_k, (0, 2, 3, 4, 1)).reshape(n_phys, -1, page)
    v_pages = cache_v.reshape(n_phys, page * cache_v.shape[2], cache_v.shape[3])
    decode = (k_pages, v_pages, page_table)
    pos_s = past_len + jnp.arange(t_s, dtype=jnp.int32)
    y_s, k_s, v_s, conv_s, h_s = _trunk(x_sample, pos_s, state_conv, state_rglru, decode, p, cfg_s)
    return (y_p, y_s, k_p, v_p, conv_p, h_p, k_s, v_s, conv_s, h_s)
```

```python
import functools
import math

import jax
import jax.numpy as jnp
from jax import lax
from jax.experimental import pallas as pl
from jax.experimental.pallas import tpu as pltpu

F32 = jnp.float32
BF16 = jnp.bfloat16

EPS = 1e-6
RG_C = 8.0
ROPE_THETA = 10000.0
CONV_WIDTH = 4
LANES = 128
SUBLANES = 8
NEG_BIG = -1e30
VMEM_LIMIT = 56 * 1024 * 1024


def _cparams(sem):
    return pltpu.CompilerParams(dimension_semantics=sem, vmem_limit_bytes=VMEM_LIMIT)


def _rmsnorm_bf16(x, g):
    ms = jnp.mean(x * x, axis=-1, keepdims=True)
    return ((x * lax.rsqrt(ms + EPS)) * g).astype(BF16)


def _nmm_kernel(*refs, n_w, n_extra, n_out, epi, has_rider):
    x_ref, g_ref = refs[0], refs[1]
    w_refs = refs[2:2 + n_w]
    e_refs = refs[2 + n_w:2 + n_w + n_extra]
    pos = 2 + n_w + n_extra
    if has_rider:
        xr_ref = refs[pos]
        pos += 1
    o_refs = refs[pos:pos + n_out]
    i, j = pl.program_id(0), pl.program_id(1)

    @pl.when(j == 0)
    def _():
        refs[-1][...] = _rmsnorm_bf16(x_ref[...], g_ref[...])

    xn = refs[-1][...]
    w_tiles = [w[...].astype(BF16) for w in w_refs]
    epi([jnp.dot(xn, w, preferred_element_type=F32) for w in w_tiles], e_refs, o_refs)

    if has_rider:
        ro_refs = refs[pos + n_out:pos + 2 * n_out]
        xrn_ref = refs[-2]

        @pl.when(i == 0)
        def _():
            @pl.when(j == 0)
            def _():
                xrn_ref[...] = _rmsnorm_bf16(xr_ref[...], g_ref[...])

            xrn = xrn_ref[...]
            epi([jnp.dot(xrn, w, preferred_element_type=F32) for w in w_tiles], e_refs, [o.at[0] for o in ro_refs])

        @pl.when(i != 0)
        def _():
            for o in ro_refs:
                o[...] = jnp.zeros(o.shape, o.dtype)


def _norm_matmul(x, g, ws, extras, outs, epi, *, tm, tn, name, seq_len=None, n=None, col_starts=None,
                 rider=None):
    m, d = x.shape
    n = ws[0].shape[1] if n is None else n
    col_starts = [0] * len(ws) if col_starts is None else col_starts
    tm = min(tm, m)
    grid = (m // tm, n // tn)
    in_specs = [pl.BlockSpec((tm, d), lambda i, j: (i, 0)),
                pl.BlockSpec((1, d), lambda i, j: (0, 0))]
    in_specs += [pl.BlockSpec((d, tn), functools.partial(lambda i, j, first: (0, first + j), first=c // tn))
                 for c in col_starts]
    in_specs += [pl.BlockSpec((tm, LANES), lambda i, j: (i, 0)) for _ in extras]
    out_specs, out_shape = [], []
    for dt, transposed in outs:
        if transposed:
            per_seq = seq_len // tm
            out_specs.append(pl.BlockSpec((1, tn, tm), lambda i, j: (i // per_seq, j, i % per_seq)))
            out_shape.append(jax.ShapeDtypeStruct((m // seq_len, n, seq_len), dt))
        else:
            out_specs.append(pl.BlockSpec((tm, tn), lambda i, j: (i, j)))
            out_shape.append(jax.ShapeDtypeStruct((m, n), dt))
    scratch = [pltpu.VMEM((tm, d), BF16)]
    operands = [x, g.reshape(1, d), *ws, *extras]
    if rider is not None:
        assert not extras and not any(transposed for _, transposed in outs)
        mr = rider.shape[0]
        in_specs.append(pl.BlockSpec((mr, d), lambda i, j: (0, 0)))
        operands.append(rider)
        for dt, _ in outs:
            out_specs.append(pl.BlockSpec((1, mr, tn), lambda i, j: (i, 0, j)))
            out_shape.append(jax.ShapeDtypeStruct((grid[0], mr, n), dt))
        scratch.insert(0, pltpu.VMEM((mr, d), BF16))
    kern = functools.partial(_nmm_kernel, n_w=len(ws), n_extra=len(extras), n_out=len(outs), epi=epi,
                             has_rider=rider is not None)
    res = pl.pallas_call(
        kern, grid=grid, in_specs=in_specs, out_specs=out_specs, out_shape=out_shape,
        scratch_shapes=scratch, compiler_params=_cparams(("arbitrary", "arbitrary")), name=name,
    )(*operands)
    if rider is not None:
        res = list(res[:len(outs)]) + [r[0] for r in res[len(outs):]]
    return res


def _epi_recurrent_in(accs, e_refs, o_refs):
    o_refs[0][...] = jax.nn.gelu(accs[0])
    o_refs[1][...] = accs[1]


def _epi_swiglu(accs, e_refs, o_refs):
    o_refs[0][...] = (jax.nn.silu(accs[0]) * accs[1]).astype(BF16)


def _rope_slab(y, cos, sin_signed):
    lane = lax.broadcasted_iota(jnp.int32, y.shape, 1)
    first_half = (lane % 64) < 32
    rot = jnp.where(first_half, pltpu.roll(y, LANES - 32, 1), pltpu.roll(y, 32, 1))
    return y * cos + rot * sin_signed


def _epi_q(accs, e_refs, o_refs, *, scale, transposed):
    cos, sin_signed = e_refs[0][...], e_refs[1][...]
    for s in range(accs[0].shape[1] // LANES):
        sl = slice(s * LANES, (s + 1) * LANES)
        q = _rope_slab(accs[0][:, sl], cos, sin_signed) * scale
        if transposed:
            o_refs[0][0, sl, :] = q.T.astype(BF16)
        else:
            o_refs[0][:, sl] = q.astype(BF16)


def _epi_kv(accs, e_refs, o_refs):
    cos, sin_signed = e_refs[0][...], e_refs[1][...]
    for s in range(accs[0].shape[1] // LANES):
        sl = slice(s * LANES, (s + 1) * LANES)
        k = _rope_slab(accs[0][:, sl], cos, sin_signed)
        o_refs[0][:, sl] = k
        o_refs[1][:, sl] = k.astype(BF16)
    o_refs[2][...] = accs[1]
    o_refs[3][...] = accs[1].astype(BF16)


def _epi_kv_transposed(accs, e_refs, o_refs):
    cos, sin_signed = e_refs[0][...], e_refs[1][...]
    for s in range(accs[0].shape[1] // LANES):
        sl = slice(s * LANES, (s + 1) * LANES)
        k = _rope_slab(accs[0][:, sl], cos, sin_signed)
        o_refs[0][0, sl, :] = k.T
        o_refs[1][:, sl] = k.astype(BF16)
        o_refs[3][0, sl, :] = accs[1][:, sl].T.astype(BF16)
    o_refs[2][...] = accs[1]


def _mm_res_kernel(a_ref, w_ref, r_ref, ar_ref, rr_ref, o_ref, or_ref):
    w = w_ref[...].astype(BF16)
    o_ref[...] = r_ref[...] + jnp.dot(a_ref[...], w, preferred_element_type=F32)
    i = pl.program_id(0)

    @pl.when(i == 0)
    def _():
        or_ref[0] = rr_ref[...] + jnp.dot(ar_ref[...], w, preferred_element_type=F32)

    @pl.when(i != 0)
    def _():
        or_ref[...] = jnp.zeros(or_ref.shape, or_ref.dtype)


def _matmul_residual(a, w, res, a_rider, res_rider, *, tm, tn, name):
    m, k = a.shape
    mr = a_rider.shape[0]
    n = w.shape[1]
    tm = min(tm, m)
    out, out_rider = pl.pallas_call(
        _mm_res_kernel, grid=(m // tm, n // tn),
        in_specs=[pl.BlockSpec((tm, k), lambda i, j: (i, 0)),
                  pl.BlockSpec((k, tn), lambda i, j: (0, j)),
                  pl.BlockSpec((tm, tn), lambda i, j: (i, j)),
                  pl.BlockSpec((mr, k), lambda i, j: (0, 0)),
                  pl.BlockSpec((mr, tn), lambda i, j: (0, j))],
        out_specs=[pl.BlockSpec((tm, tn), lambda i, j: (i, j)),
                   pl.BlockSpec((1, mr, tn), lambda i, j: (i, 0, j))],
        out_shape=[jax.ShapeDtypeStruct((m, n), F32), jax.ShapeDtypeStruct((m // tm, mr, n), F32)],
        compiler_params=_cparams(("arbitrary", "arbitrary")), name=name,
    )(a, w, res, a_rider, res_rider)
    return out, out_rider[0]


def _rmsnorm_kernel(x_ref, g_ref, o_ref):
    x = x_ref[...]
    ms = jnp.mean(x * x, axis=-1, keepdims=True)
    o_ref[...] = (x * lax.rsqrt(ms + EPS)) * g_ref[...]


def _rmsnorm(x, g, *, tm):
    m, d = x.shape
    tm = min(tm, m)
    return pl.pallas_call(
        _rmsnorm_kernel, grid=(m // tm,),
        in_specs=[pl.BlockSpec((tm, d), lambda i: (i, 0)), pl.BlockSpec((1, d), lambda i: (0, 0))],
        out_specs=pl.BlockSpec((tm, d), lambda i: (i, 0)),
        out_shape=jax.ShapeDtypeStruct((m, d), F32),
        compiler_params=_cparams(("parallel",)), name="final_rmsnorm",
    )(x, g.reshape(1, d))


_EXPM1_SERIES_BOUND = 2.0 ** -6
_EXPM1_SERIES_TERMS = 4


def _neg_expm1(x):
    poly = jnp.full_like(x, 1.0 / math.factorial(_EXPM1_SERIES_TERMS))
    for k in range(_EXPM1_SERIES_TERMS - 1, 0, -1):
        poly = poly * x + 1.0 / math.factorial(k)
    return jnp.where(x > -_EXPM1_SERIES_BOUND, -(poly * x), 1.0 - jnp.exp(x))


def _scan_kernel(u_ref, gate_ref, cw_ref, cb_ref, wa_ref, ba_ref, wi_ref, bi_ref, lam_ref, conv0_ref, h0_ref,
                 hg_ref, convout_ref, hlast_ref, ubuf, a_s, b_s, hs, hcar, *, tc, last_row, n_gate_blocks):
    c = pl.program_id(1)
    width = u_ref.shape[2]
    pad = SUBLANES
    hist = CONV_WIDTH - 1

    @pl.when(c == 0)
    def _():
        ubuf[0:pad, :] = jnp.zeros((pad, width), F32)
        ubuf[pad - hist:pad, :] = conv0_ref[0]
        hcar[...] = jnp.broadcast_to(h0_ref[0], (SUBLANES, width))

    u = u_ref[0]
    ubuf[pad:pad + tc, :] = u
    cw = cw_ref[...]
    xc = cb_ref[...] + u * cw[0:1, :]
    for j in range(1, CONV_WIDTH):
        xc = xc + ubuf[pad - j:pad - j + tc, :] * cw[j:j + 1, :]

    neg_c_softplus = -RG_C * jax.nn.softplus(-lam_ref[...])
    gb = width // n_gate_blocks
    for n in range(n_gate_blocks):
        sl = slice(n * gb, (n + 1) * gb)
        xs = xc[:, sl]
        xb = xs.astype(BF16)
        r = jax.nn.sigmoid(jnp.dot(xb, wa_ref[n].astype(BF16), preferred_element_type=F32) + ba_ref[:, sl])
        i = jax.nn.sigmoid(jnp.dot(xb, wi_ref[n].astype(BF16), preferred_element_type=F32) + bi_ref[:, sl])
        log_a = r * neg_c_softplus[:, sl]
        a_s[:, sl] = jnp.exp(log_a)
        b_s[:, sl] = jnp.sqrt(_neg_expm1(2.0 * log_a)) * i * xs

    row = lax.broadcasted_iota(jnp.int32, (SUBLANES, width), 0)

    def group(g, h):
        base = pl.multiple_of(g * SUBLANES, SUBLANES)
        a8 = a_s[pl.ds(base, SUBLANES), :]
        b8 = b_s[pl.ds(base, SUBLANES), :]
        out = jnp.zeros((SUBLANES, width), F32)
        for r in range(SUBLANES):
            hn = a8 * h + b8
            out = jnp.where(row == r, hn, out)
            h = jnp.broadcast_to(hn[r:r + 1, :], (SUBLANES, width))
        hs[pl.ds(base, SUBLANES), :] = out
        return h

    hcar[...] = lax.fori_loop(0, tc // SUBLANES, group, hcar[...])
    hg_ref[0] = (hs[...] * gate_ref[0]).astype(BF16)
    ubuf[0:pad, :] = ubuf[tc:tc + pad, :]

    @pl.when(c == pl.num_programs(1) - 1)
    def _():
        convout_ref[0] = ubuf[pad + last_row + 1 - hist:pad + last_row + 1, :]
        hlast_ref[0] = hs[last_row:last_row + 1, :]


def _conv_rglru(u, gate, conv0, h0, cw, cb, wa, ba, wi, bi, lam, *, tc, t_valid):
    b, t, width = u.shape
    tc = min(tc, t)
    n_chunks = t // tc
    last_row = (t_valid - 1) - (n_chunks - 1) * tc
    nb = wa.shape[0]
    row2 = lambda v: v.reshape(1, width)
    kern = functools.partial(_scan_kernel, tc=tc, last_row=last_row, n_gate_blocks=nb)
    full2 = lambda shape: pl.BlockSpec(shape, lambda i, c: (0, 0))
    return pl.pallas_call(
        kern, grid=(b, n_chunks),
        in_specs=[pl.BlockSpec((1, tc, width), lambda i, c: (i, c, 0)),
                  pl.BlockSpec((1, tc, width), lambda i, c: (i, c, 0)),
                  full2((CONV_WIDTH, width)), full2((1, width)),
                  pl.BlockSpec(wa.shape, lambda i, c: (0, 0, 0)), full2((1, width)),
                  pl.BlockSpec(wi.shape, lambda i, c: (0, 0, 0)), full2((1, width)),
                  full2((1, width)),
                  pl.BlockSpec((1, CONV_WIDTH - 1, width), lambda i, c: (i, 0, 0)),
                  pl.BlockSpec((1, 1, width), lambda i, c: (i, 0, 0))],
        out_specs=[pl.BlockSpec((1, tc, width), lambda i, c: (i, c, 0)),
                   pl.BlockSpec((1, CONV_WIDTH - 1, width), lambda i, c: (i, 0, 0)),
                   pl.BlockSpec((1, 1, width), lambda i, c: (i, 0, 0))],
        out_shape=[jax.ShapeDtypeStruct((b, t, width), BF16),
                   jax.ShapeDtypeStruct((b, CONV_WIDTH - 1, width), F32),
                   jax.ShapeDtypeStruct((b, 1, width), F32)],
        scratch_shapes=[pltpu.VMEM((tc + SUBLANES, width), F32),
                        pltpu.VMEM((tc, width), F32), pltpu.VMEM((tc, width), F32),
                        pltpu.VMEM((tc, width), F32), pltpu.VMEM((SUBLANES, width), F32)],
        compiler_params=_cparams(("parallel", "arbitrary")), name="conv_rglru",
    )(u, gate, cw, row2(cb), wa, row2(ba), wi, row2(bi), row2(lam), conv0, h0.reshape(b, 1, width))


def _diff_lambda(lq1_ref, lk1_ref, lq2_ref, lk2_ref, lam_init):
    e1 = jnp.exp(jnp.sum(lq1_ref[...] * lk1_ref[...], axis=-1, keepdims=True))
    e2 = jnp.exp(jnp.sum(lq2_ref[...] * lk2_ref[...], axis=-1, keepdims=True))
    return e1 - e2 + lam_init


LOG2_E = math.log2(math.e)


def _online_softmax_step(s, v, m_ref, l_ref, acc_ref):
    m_old = m_ref[...]
    m_new = jnp.maximum(m_old, jnp.max(s, axis=-1, keepdims=True))
    alpha = jnp.exp2(m_old - m_new)
    p = jnp.exp2(s - m_new)
    l_ref[...] = alpha * l_ref[...] + jnp.sum(p, axis=-1, keepdims=True)
    acc_ref[...] = alpha * acc_ref[...] + jnp.dot(p.astype(BF16), v, preferred_element_type=F32)
    m_ref[...] = m_new


def _head_out(n0, n1, lam, subln, out_scale):
    o = n0 - lam * n1
    ms = jnp.mean(o * o, axis=-1, keepdims=True)
    return ((o * lax.rsqrt(ms + EPS)) * subln) * out_scale


def _attn_prompt_kernel(qidx_ref, kidx_ref, lq1_ref, lk1_ref, lq2_ref, lk2_ref, subln_ref, bias_ref,
                        qt_ref, k_ref, vt_ref, o_ref, qs_ref, s_ref, cmax_ref, m_ref, l_ref, acc_ref, *,
                        tq, n_items, lam_init):
    n_q = qt_ref.shape[2] // tq
    feat = lax.broadcasted_iota(jnp.int32, (LANES, tq), 0)
    zero = jnp.zeros((LANES, tq), BF16)
    half = LANES // 2
    for qi in range(n_q):
        qt = qt_ref[0, :, qi * tq:(qi + 1) * tq]
        qs_ref[:, (2 * qi) * tq:(2 * qi + 1) * tq] = jnp.where(feat < half, qt, zero)
        qs_ref[:, (2 * qi + 1) * tq:(2 * qi + 2) * tq] = jnp.where(feat >= half, qt, zero)
    m_ref[...] = jnp.full(m_ref.shape, NEG_BIG, F32)
    l_ref[...] = jnp.zeros(l_ref.shape, F32)
    acc_ref[...] = jnp.zeros(acc_ref.shape, F32)
    lam = _diff_lambda(lq1_ref, lk1_ref, lq2_ref, lk2_ref, lam_init)

    def produce(w, slot):
        qi, ki = qidx_ref[w], kidx_ref[w]
        kstart = pl.multiple_of(ki * tq, tq)
        qstart = pl.multiple_of(qi * (2 * tq), 2 * tq)
        s = jnp.dot(k_ref[0, pl.ds(kstart, tq), :], qs_ref[:, pl.ds(qstart, 2 * tq)],
                    preferred_element_type=F32)
        s = s + bias_ref[(qi == ki).astype(jnp.int32)]
        s_ref[slot] = s
        cmax_ref[slot] = jnp.max(s, axis=0, keepdims=True)

    def consume(w, slot):
        qi, ki = qidx_ref[w], kidx_ref[w]
        kstart = pl.multiple_of(ki * tq, tq)
        m_old = jnp.where(ki == 0, NEG_BIG, m_ref[...])
        m_new = jnp.maximum(m_old, cmax_ref[slot])
        alpha = jnp.exp2(m_old - m_new)
        p = jnp.exp2(s_ref[slot] - m_new)
        l_ref[qi] = alpha * l_ref[qi] + jnp.sum(p, axis=0, keepdims=True)
        pv = jnp.dot(vt_ref[0, :, pl.ds(kstart, tq)], p.astype(BF16), preferred_element_type=F32)
        acc_ref[qi] = alpha * acc_ref[qi] + pv
        m_ref[...] = m_new

    produce(0, 0)
    n_looped = max(n_items // 2 - 1, 0)

    def pair(i, carry):
        produce(2 * i + 1, 1)
        consume(2 * i, 0)
        produce(2 * i + 2, 0)
        consume(2 * i + 1, 1)
        return carry

    lax.fori_loop(0, n_looped, pair, 0)
    for w in range(2 * n_looped, n_items):
        if w + 1 < n_items:
            produce(w + 1, (w + 1) % 2)
        consume(w, w % 2)

    for qi in range(n_q):
        n = acc_ref[qi] / l_ref[qi]
        o = n[:, 0:tq] - lam * n[:, tq:2 * tq]
        ms = jnp.mean(o * o, axis=0, keepdims=True)
        y = ((o * lax.rsqrt(ms + EPS)) * subln_ref[...]) * (1.0 - lam_init)
        o_ref[0, qi * tq:(qi + 1) * tq, :] = y.T.astype(BF16)


def _attn_prompt(qt, k, vt, lq1, lk1, lq2, lk2, subln, lam_init, *, tq):
    b, hw, t = qt.shape
    nh = hw // LANES
    assert t % tq == 0
    n_q = t // tq
    items = [(qi, ki) for qi in range(n_q) for ki in range(qi + 1)]
    qidx = jnp.array([qi for qi, _ in items], jnp.int32)
    kidx = jnp.array([ki for _, ki in items], jnp.int32)
    kpos = lax.broadcasted_iota(jnp.int32, (tq, 2 * tq), 0)
    qpos = lax.broadcasted_iota(jnp.int32, (tq, 2 * tq), 1) % tq
    bias = jnp.stack([jnp.zeros((tq, 2 * tq), F32), jnp.where(kpos <= qpos, 0.0, NEG_BIG).astype(F32)])
    vec = lambda a: a.reshape(1, -1)
    small = lambda w: pl.BlockSpec((1, w), lambda i, h, *_: (0, 0))
    kern = functools.partial(_attn_prompt_kernel, tq=tq, n_items=len(items), lam_init=lam_init)
    grid_spec = pltpu.PrefetchScalarGridSpec(
        num_scalar_prefetch=2, grid=(b, nh),
        in_specs=[small(lq1.size), small(lk1.size), small(lq2.size), small(lk2.size),
                  pl.BlockSpec((LANES, 1), lambda i, h, *_: (0, 0)),
                  pl.BlockSpec((2, tq, 2 * tq), lambda i, h, *_: (0, 0, 0)),
                  pl.BlockSpec((1, LANES, t), lambda i, h, *_: (i, h, 0)),
                  pl.BlockSpec((1, t, LANES), lambda i, h, *_: (i, 0, h)),
                  pl.BlockSpec((1, LANES, t), lambda i, h, *_: (i, h, 0))],
        out_specs=pl.BlockSpec((1, t, LANES), lambda i, h, *_: (i, 0, h)),
        scratch_shapes=[pltpu.VMEM((LANES, 2 * t), BF16), pltpu.VMEM((2, tq, 2 * tq), F32),
                        pltpu.VMEM((2, 1, 2 * tq), F32), pltpu.VMEM((1, 2 * tq), F32),
                        pltpu.VMEM((n_q, 1, 2 * tq), F32), pltpu.VMEM((n_q, LANES, 2 * tq), F32)])
    return pl.pallas_call(
        kern, grid_spec=grid_spec, out_shape=jax.ShapeDtypeStruct((b, t, hw), BF16),
        compiler_params=_cparams(("parallel", "parallel")), name="diff_attn_prompt",
    )(qidx, kidx, vec(lq1), vec(lk1), vec(lq2), vec(lk2), subln.reshape(LANES, 1), bias, qt, k, vt)


DEC_ROWS = 2 * SUBLANES


def _attn_decode_kernel(pt_ref, lq1_ref, lk1_ref, lq2_ref, lk2_ref, subln_ref, q_ref, kn_ref, vn_ref, *rest,
                        pages, n_heads, t_new, lam_init):
    k_refs = rest[:pages]
    v_refs = rest[pages:2 * pages]
    o_ref = rest[2 * pages]
    m_ref, l_ref, acc_ref, vh_ref = rest[2 * pages + 1:]
    j = pl.program_id(1)

    @pl.when(j == 0)
    def _():
        m_ref[...] = jnp.full(m_ref.shape, NEG_BIG, F32)
        l_ref[...] = jnp.zeros(l_ref.shape, F32)
        acc_ref[...] = jnp.zeros(acc_ref.shape, F32)

    page = k_refs[0].shape[2]
    for p, vr in enumerate(v_refs):
        v_heads = jnp.swapaxes(vr[0].reshape(page, n_heads, LANES), 0, 1)
        vh_ref[:, p * page:(p + 1) * page, :] = v_heads.astype(BF16)
    head_lanes = lambda h: slice(h * LANES, (h + 1) * LANES)
    head_rows = lambda h: slice(h * DEC_ROWS, (h + 1) * DEC_ROWS)

    def softmax_step(s_heads, v_of_head):
        s = jnp.concatenate(s_heads, axis=0)
        m_old = m_ref[...]
        m_new = jnp.maximum(m_old, jnp.max(s, axis=-1, keepdims=True))
        alpha = jnp.exp2(m_old - m_new)
        p = jnp.exp2(s - m_new)
        l_ref[...] = alpha * l_ref[...] + jnp.sum(p, axis=-1, keepdims=True)
        pb = p.astype(BF16)
        pv = [jnp.dot(pb[head_rows(h)], v_of_head(h), preferred_element_type=F32) for h in range(n_heads)]
        acc_ref[...] = alpha * acc_ref[...] + jnp.concatenate(pv, axis=0)
        m_ref[...] = m_new

    s_heads = [jnp.dot(q_ref[0, h],
                       jnp.concatenate([kr[0, head_lanes(h), :].astype(BF16) for kr in k_refs], axis=1),
                       preferred_element_type=F32) for h in range(n_heads)]
    softmax_step(s_heads, lambda h: vh_ref[h])

    @pl.when(j == pl.num_programs(1) - 1)
    def _():
        lam = _diff_lambda(lq1_ref, lk1_ref, lq2_ref, lk2_ref, lam_init)
        n_new = kn_ref.shape[1]
        qt = lax.broadcasted_iota(jnp.int32, (DEC_ROWS, n_new), 0) % SUBLANES
        kt_pos = lax.broadcasted_iota(jnp.int32, (DEC_ROWS, n_new), 1)
        visible = (kt_pos <= qt) & (kt_pos < t_new)
        s_new = [jnp.where(visible,
                           lax.dot_general(q_ref[0, h], kn_ref[0, :, head_lanes(h)], (((1,), (1,)), ((), ())),
                                           preferred_element_type=F32), NEG_BIG) for h in range(n_heads)]
        softmax_step(s_new, lambda h: vn_ref[0, :, head_lanes(h)])
        n = acc_ref[...] / l_ref[...]
        for h in range(n_heads):
            nh_ = n[head_rows(h)]
            o_ref[0, :, head_lanes(h)] = _head_out(nh_[0:SUBLANES], nh_[SUBLANES:DEC_ROWS], lam, subln_ref[...],
                                                   1.0 - lam_init)


def _attn_decode(qd, k_new, v_new, cache_k, cache_v, page_table, lq1, lk1, lq2, lk2, subln, lam_init, *,
                 pages, t_new):
    b, nh = qd.shape[0], qd.shape[1]
    hw = nh * LANES
    n_pages = page_table.shape[1]
    assert n_pages % pages == 0
    vec = lambda a: a.reshape(1, -1)
    small = lambda w: pl.BlockSpec((1, w), lambda i, j, pt: (0, 0))

    def page_spec(cache, p):
        return pl.BlockSpec((1,) + cache.shape[1:], lambda i, j, pt: (pt[i, j * pages + p], 0, 0))

    kern = functools.partial(_attn_decode_kernel, pages=pages, n_heads=nh, t_new=t_new, lam_init=lam_init)
    grid_spec = pltpu.PrefetchScalarGridSpec(
        num_scalar_prefetch=1, grid=(b, n_pages // pages),
        in_specs=[small(lq1.size), small(lk1.size), small(lq2.size), small(lk2.size), small(LANES),
                  pl.BlockSpec((1, nh, DEC_ROWS, LANES), lambda i, j, pt: (i, 0, 0, 0)),
                  pl.BlockSpec((1, k_new.shape[1], hw), lambda i, j, pt: (i, 0, 0)),
                  pl.BlockSpec((1, v_new.shape[1], hw), lambda i, j, pt: (i, 0, 0))]
                 + [page_spec(cache_k, p) for p in range(pages)] + [page_spec(cache_v, p) for p in range(pages)],
        out_specs=pl.BlockSpec((1, SUBLANES, hw), lambda i, j, pt: (i, 0, 0)),
        scratch_shapes=[pltpu.VMEM((nh * DEC_ROWS, 1), F32), pltpu.VMEM((nh * DEC_ROWS, 1), F32),
                        pltpu.VMEM((nh * DEC_ROWS, LANES), F32),
                        pltpu.VMEM((nh, pages * cache_k.shape[2], LANES), BF16)])
    return pl.pallas_call(
        kern, grid_spec=grid_spec, out_shape=jax.ShapeDtypeStruct((b, SUBLANES, hw), F32),
        compiler_params=_cparams(("parallel", "arbitrary")), name="diff_attn_decode",
    )(page_table, vec(lq1), vec(lk1), vec(lq2), vec(lk2), vec(subln), qd, k_new, v_new,
      *([cache_k] * pages), *([cache_v] * pages))


def _rope_tables(pos, head_dim, batch):
    half = head_dim // 2
    inv = ROPE_THETA ** (-jnp.arange(half, dtype=F32) / half)
    ang = pos.astype(F32)[:, None] * inv[None, :]
    cos, sin = jnp.cos(ang), jnp.sin(ang)
    reps = LANES // head_dim
    cos_t = jnp.tile(jnp.concatenate([cos, cos], axis=1), (batch, reps))
    sin_t = jnp.tile(jnp.concatenate([-sin, sin], axis=1), (batch, reps))
    return cos_t, sin_t


def _plain(*dtypes):
    return [(dt, False) for dt in dtypes]


class _Group:
    def __init__(self, x, pos, conv0, h0, decode, cfg, head_dim):
        self.b, self.t, d = x.shape
        self.m = self.b * self.t
        self.x = x.reshape(self.m, d)
        self.cos, self.sin = _rope_tables(pos, head_dim, self.b)
        self.conv0, self.h0, self.decode, self.cfg = conv0, h0, decode, cfg
        self.prompt = decode is None
        self.new_bufs, self.new_hs = [], []
        self.kb = self.vb = self.k_out = self.v_new = None

    def mixer_recurrent(self, gate, u, l, p):
        b, t = self.b, self.t
        width = u.shape[1]
        t_pad = -(-t // SUBLANES) * SUBLANES
        u3, g3 = u.reshape(b, t, width), gate.reshape(b, t, width)
        if t_pad != t:
            u3 = jnp.pad(u3, ((0, 0), (0, t_pad - t), (0, 0)))
            g3 = jnp.pad(g3, ((0, 0), (0, t_pad - t), (0, 0)))
        hg, nb, nh = _conv_rglru(u3, g3, self.conv0[l], self.h0[l], p['rg_conv_w'][l], p['rg_conv_b'][l],
                                 p['rg_w_a'][l], p['rg_b_a'][l], p['rg_w_i'][l], p['rg_b_i'][l],
                                 p['rg_lambda'][l], tc=self.cfg['tc'], t_valid=t)
        self.new_bufs.append(nb)
        self.new_hs.append(nh.reshape(b, width))
        return hg[:, :t].reshape(self.m, width)

    def shared_kv(self, p, n_heads, head_dim):
        b, t, cfg = self.b, self.t, self.cfg
        qk_width = n_heads * 2 * head_dim
        assert p['w_kv'].shape[1] == 2 * qk_width
        kv_ws = dict(ws=[p['w_kv'], p['w_kv']], n=qk_width, col_starts=[0, qk_width])
        common = dict(extras=[self.cos, self.sin], tm=cfg['tm'], tn=cfg['tn'], name="shared_kv", **kv_ws)
        if self.prompt:
            k_t, self.kb, self.v_new, self.vb = _norm_matmul(
                self.x, p['kv_norm'], outs=[(F32, True), (BF16, False), (F32, False), (BF16, True)],
                epi=_epi_kv_transposed, seq_len=t, **common)
            self.k_out = k_t.reshape(b, n_heads, 2, head_dim, t).transpose(0, 4, 1, 2, 3)
        else:
            k_new, self.kb, self.v_new, self.vb = _norm_matmul(
                self.x, p['kv_norm'], outs=_plain(F32, BF16, F32, BF16), epi=_epi_kv, **common)
            self.k_out = k_new.reshape(b, t, n_heads, 2, head_dim)

    def mixer_attention(self, l, j, p, n_heads, head_dim):
        b, t, m, cfg = self.b, self.t, self.m, self.cfg
        lam_init = 0.8 - 0.6 * math.exp(-0.3 * l)
        (q,) = _norm_matmul(self.x, p['norm_mix'][l], [p['dif_w_q'][j]], [self.cos, self.sin],
                            [(BF16, self.prompt)],
                            functools.partial(_epi_q, scale=head_dim ** -0.5 * LOG2_E, transposed=self.prompt),
                            tm=cfg['tm'], tn=cfg['tn'], name="q_proj", seq_len=t)
        lams = (p['dif_lq1'][j], p['dif_lk1'][j], p['dif_lq2'][j], p['dif_lk2'][j], p['dif_subln'][j])
        if self.prompt:
            o = _attn_prompt(q, self.kb.reshape(b, t, -1), self.vb, *lams, lam_init, tq=cfg['tq'])
            return o.reshape(m, -1)
        cache_k, cache_v, page_table = self.decode
        q4 = q.reshape(b, t, n_heads, LANES).transpose(0, 2, 1, 3)
        lane = jnp.arange(LANES)
        zpad = jnp.zeros((b, n_heads, SUBLANES - t, LANES), BF16)
        qd = jnp.concatenate([jnp.where(lane < head_dim, q4, 0).astype(BF16), zpad,
                              jnp.where(lane >= head_dim, q4, 0).astype(BF16), zpad], axis=2)
        pad_new = lambda a: jnp.pad(a.reshape(b, t, -1), ((0, 0), (0, DEC_ROWS - t), (0, 0)))
        o = _attn_decode(qd, pad_new(self.kb), pad_new(self.vb), cache_k, cache_v, page_table, *lams, lam_init,
                         pages=cfg['pages'], t_new=t)
        return o[:, :t].reshape(m, -1).astype(BF16)

    def outputs(self, p, n_heads):
        d = self.x.shape[1]
        y = _rmsnorm(self.x, p['norm_final'], tm=self.cfg['tm_norm']).reshape(self.b, self.t, d)
        v_out = self.v_new.reshape(self.b, self.t, n_heads, -1)
        return y, self.k_out, v_out, jnp.stack(self.new_bufs), jnp.stack(self.new_hs)


def _trunks(main, rider, p):
    depth = p['norm_mix'].shape[0]
    n_a = p['rg_w_x'].shape[0]
    head_dim = p['dif_lq1'].shape[1]
    n_heads = p['dif_w_q'].shape[2] // (2 * head_dim)
    cfg = main.cfg
    tm, tn = cfg['tm'], cfg['tn']
    for l in range(depth):
        if l < n_a:
            gate, u, gate_r, u_r = _norm_matmul(
                main.x, p['norm_mix'][l], [p['rg_w_gate'][l], p['rg_w_x'][l]], [], _plain(F32, F32),
                _epi_recurrent_in, tm=tm, tn=tn, name="recurrent_in", rider=rider.x)
            mixed, mixed_r = main.mixer_recurrent(gate, u, l, p), rider.mixer_recurrent(gate_r, u_r, l, p)
            w_out = p['rg_w_out'][l]
        else:
            j = l - n_a
            if j == 0:
                main.shared_kv(p, n_heads, head_dim)
                rider.shared_kv(p, n_heads, head_dim)
            mixed = main.mixer_attention(l, j, p, n_heads, head_dim)
            mixed_r = rider.mixer_attention(l, j, p, n_heads, head_dim)
            w_out = p['dif_w_o'][j]
        main.x, rider.x = _matmul_residual(mixed, w_out, main.x, mixed_r, rider.x, tm=tm, tn=tn, name="mixer_out")
        hmid, hmid_r = _norm_matmul(main.x, p['norm_ffn'][l], [p['ffn_w_gate'][l], p['ffn_w_up'][l]], [],
                                    _plain(BF16), _epi_swiglu, tm=tm, tn=tn, name="ffn_in", rider=rider.x)
        main.x, rider.x = _matmul_residual(hmid, p['ffn_w_down'][l], main.x, hmid_r, rider.x,
                                           tm=cfg['tm_down'], tn=tn, name="ffn_down")
    return main.outputs(p, n_heads), rider.outputs(p, n_heads)


def kernel(x_prompt, x_sample, cache_k, cache_v, page_table, state_conv, state_rglru, norm_mix, norm_ffn, norm_final, rg_w_x, rg_w_gate, rg_conv_w, rg_conv_b, rg_w_a, rg_b_a, rg_w_i, rg_b_i, rg_lambda, rg_w_out, kv_norm, w_kv, dif_w_q, dif_lq1, dif_lk1, dif_lq2, dif_lk2, dif_subln, dif_w_o, ffn_w_gate, ffn_w_up, ffn_w_down):
    p = dict(norm_mix=norm_mix, norm_ffn=norm_ffn, norm_final=norm_final, rg_w_x=rg_w_x,
             rg_w_gate=rg_w_gate, rg_conv_w=rg_conv_w, rg_conv_b=rg_conv_b, rg_w_a=rg_w_a,
             rg_b_a=rg_b_a, rg_w_i=rg_w_i, rg_b_i=rg_b_i, rg_lambda=rg_lambda, rg_w_out=rg_w_out,
             kv_norm=kv_norm, w_kv=w_kv, dif_w_q=dif_w_q, dif_lq1=dif_lq1, dif_lk1=dif_lk1,
             dif_lq2=dif_lq2, dif_lk2=dif_lk2, dif_subln=dif_subln, dif_w_o=dif_w_o,
             ffn_w_gate=ffn_w_gate, ffn_w_up=ffn_w_up, ffn_w_down=ffn_w_down)
    p['ffn_w_down'] = ffn_w_down.astype(BF16)

    n_a = rg_w_x.shape[0]
    width = rg_w_x.shape[2]
    head_dim = dif_lq1.shape[1]
    b_p, t_p = x_prompt.shape[0], x_prompt.shape[1]
    cfg_p = dict(tm=1024, tn=512, tm_down=1024, tm_norm=512, tc=256, tq=512)
    conv0 = jnp.zeros((n_a, b_p, CONV_WIDTH - 1, width), F32)
    h0 = jnp.zeros((n_a, b_p, width), F32)
    prompt = _Group(x_prompt, jnp.arange(t_p, dtype=jnp.int32), conv0, h0, None, cfg_p, head_dim)

    b_s, t_s = x_sample.shape[0], x_sample.shape[1]
    n_phys, page = cache_k.shape[0], cache_k.shape[1]
    past_len = page_table.shape[1] * page
    cfg_s = dict(tm=b_s * t_s, tn=512, tm_norm=b_s * t_s, tc=SUBLANES, pages=8)
    k_pages = jnp.transpose(cache_k, (0, 2, 3, 4, 1)).reshape(n_phys, -1, page)
    v_pages = cache_v.reshape(n_phys, page * cache_v.shape[2], cache_v.shape[3])
    pos_s = past_len + jnp.arange(t_s, dtype=jnp.int32)
    sample = _Group(x_sample, pos_s, state_conv, state_rglru, (k_pages, v_pages, page_table), cfg_s, head_dim)

    (y_p, k_p, v_p, conv_p, h_p), (y_s, k_s, v_s, conv_s, h_s) = _trunks(prompt, sample, p)
    return (y_p, y_s, k_p, v_p, conv_p, h_p, k_s, v_s, conv_s, h_s)
```

```python
import functools
import math

import jax
import jax.numpy as jnp
from jax import lax
from jax.experimental import pallas as pl
from jax.experimental.pallas import tpu as pltpu

F32 = jnp.float32
BF16 = jnp.bfloat16

EPS = 1e-6
RG_C = 8.0
ROPE_THETA = 10000.0
CONV_WIDTH = 4
LANES = 128
SUBLANES = 8
NEG_BIG = -1e30
VMEM_LIMIT = 56 * 1024 * 1024


def _cparams(sem):
    return pltpu.CompilerParams(dimension_semantics=sem, vmem_limit_bytes=VMEM_LIMIT)


def _weight_operand(w, rows, tn, first_block=0):
    if isinstance(w, tuple):
        stacked, layer = w
        return stacked, pl.BlockSpec((None, rows, tn), lambda i, j: (layer, 0, first_block + j))
    return w, pl.BlockSpec((rows, tn), lambda i, j: (0, first_block + j))


def _weight_cols(w):
    return w[0].shape[2] if isinstance(w, tuple) else w.shape[1]


def _rmsnorm_bf16(x, g):
    ms = jnp.mean(x * x, axis=-1, keepdims=True)
    return ((x * lax.rsqrt(ms + EPS)) * g).astype(BF16)


def _nmm_kernel(*refs, n_w, n_extra, n_out, epi, has_rider):
    x_ref, g_ref = refs[0], refs[1]
    w_refs = refs[2:2 + n_w]
    e_refs = refs[2 + n_w:2 + n_w + n_extra]
    pos = 2 + n_w + n_extra
    if has_rider:
        xr_ref = refs[pos]
        pos += 1
    o_refs = refs[pos:pos + n_out]
    i, j = pl.program_id(0), pl.program_id(1)

    @pl.when(j == 0)
    def _():
        refs[-1][...] = _rmsnorm_bf16(x_ref[...], g_ref[...])

    xn = refs[-1][...]
    w_tiles = [w[...].astype(BF16) for w in w_refs]
    epi([jnp.dot(xn, w, preferred_element_type=F32) for w in w_tiles], e_refs, o_refs)

    if has_rider:
        ro_refs = refs[pos + n_out:pos + 2 * n_out]
        xrn_ref = refs[-2]

        @pl.when(i == 0)
        def _():
            @pl.when(j == 0)
            def _():
                xrn_ref[...] = _rmsnorm_bf16(xr_ref[...], g_ref[...])

            xrn = xrn_ref[...]
            epi([jnp.dot(xrn, w, preferred_element_type=F32) for w in w_tiles], e_refs, [o.at[0] for o in ro_refs])

        @pl.when(i != 0)
        def _():
            for o in ro_refs:
                o[...] = jnp.zeros(o.shape, o.dtype)


def _norm_matmul(x, g, ws, extras, outs, epi, *, tm, tn, name, seq_len=None, n=None, col_starts=None,
                 rider=None):
    m, d = x.shape
    n = _weight_cols(ws[0]) if n is None else n
    col_starts = [0] * len(ws) if col_starts is None else col_starts
    tm = min(tm, m)
    grid = (m // tm, n // tn)
    in_specs = [pl.BlockSpec((tm, d), lambda i, j: (i, 0)),
                pl.BlockSpec((1, d), lambda i, j: (0, 0))]
    w_arrays = []
    for w, c in zip(ws, col_starts):
        arr, spec = _weight_operand(w, d, tn, c // tn)
        w_arrays.append(arr)
        in_specs.append(spec)
    in_specs += [pl.BlockSpec((tm, LANES), lambda i, j: (i, 0)) for _ in extras]
    out_specs, out_shape = [], []
    for dt, transposed in outs:
        if transposed:
            per_seq = seq_len // tm
            out_specs.append(pl.BlockSpec((1, tn, tm), lambda i, j: (i // per_seq, j, i % per_seq)))
            out_shape.append(jax.ShapeDtypeStruct((m // seq_len, n, seq_len), dt))
        else:
            out_specs.append(pl.BlockSpec((tm, tn), lambda i, j: (i, j)))
            out_shape.append(jax.ShapeDtypeStruct((m, n), dt))
    scratch = [pltpu.VMEM((tm, d), BF16)]
    operands = [x, g.reshape(1, d), *w_arrays, *extras]
    if rider is not None:
        assert not extras and not any(transposed for _, transposed in outs)
        mr = rider.shape[0]
        in_specs.append(pl.BlockSpec((mr, d), lambda i, j: (0, 0)))
        operands.append(rider)
        for dt, _ in outs:
            out_specs.append(pl.BlockSpec((1, mr, tn), lambda i, j: (i, 0, j)))
            out_shape.append(jax.ShapeDtypeStruct((grid[0], mr, n), dt))
        scratch.insert(0, pltpu.VMEM((mr, d), BF16))
    kern = functools.partial(_nmm_kernel, n_w=len(ws), n_extra=len(extras), n_out=len(outs), epi=epi,
                             has_rider=rider is not None)
    res = pl.pallas_call(
        kern, grid=grid, in_specs=in_specs, out_specs=out_specs, out_shape=out_shape,
        scratch_shapes=scratch, compiler_params=_cparams(("arbitrary", "arbitrary")), name=name,
    )(*operands)
    if rider is not None:
        res = list(res[:len(outs)]) + [r[0] for r in res[len(outs):]]
    return res


def _epi_recurrent_in(accs, e_refs, o_refs):
    o_refs[0][...] = jax.nn.gelu(accs[0])
    o_refs[1][...] = accs[1]


def _epi_swiglu(accs, e_refs, o_refs):
    o_refs[0][...] = (jax.nn.silu(accs[0]) * accs[1]).astype(BF16)


def _rope_slab(y, cos, sin_signed):
    lane = lax.broadcasted_iota(jnp.int32, y.shape, 1)
    first_half = (lane % 64) < 32
    rot = jnp.where(first_half, pltpu.roll(y, LANES - 32, 1), pltpu.roll(y, 32, 1))
    return y * cos + rot * sin_signed


def _epi_q(accs, e_refs, o_refs, *, scale, transposed):
    cos, sin_signed = e_refs[0][...], e_refs[1][...]
    for s in range(accs[0].shape[1] // LANES):
        sl = slice(s * LANES, (s + 1) * LANES)
        q = _rope_slab(accs[0][:, sl], cos, sin_signed) * scale
        if transposed:
            o_refs[0][0, sl, :] = q.T.astype(BF16)
        else:
            o_refs[0][:, sl] = q.astype(BF16)


def _epi_kv(accs, e_refs, o_refs):
    cos, sin_signed = e_refs[0][...], e_refs[1][...]
    for s in range(accs[0].shape[1] // LANES):
        sl = slice(s * LANES, (s + 1) * LANES)
        k = _rope_slab(accs[0][:, sl], cos, sin_signed)
        o_refs[0][:, sl] = k
        o_refs[1][:, sl] = k.astype(BF16)
    o_refs[2][...] = accs[1]
    o_refs[3][...] = accs[1].astype(BF16)


def _epi_kv_transposed(accs, e_refs, o_refs):
    cos, sin_signed = e_refs[0][...], e_refs[1][...]
    for s in range(accs[0].shape[1] // LANES):
        sl = slice(s * LANES, (s + 1) * LANES)
        k = _rope_slab(accs[0][:, sl], cos, sin_signed)
        o_refs[0][0, sl, :] = k.T
        o_refs[1][:, sl] = k.astype(BF16)
        o_refs[3][0, sl, :] = accs[1][:, sl].T.astype(BF16)
    o_refs[2][...] = accs[1]


def _mm_res_kernel(a_ref, w_ref, r_ref, ar_ref, rr_ref, o_ref, or_ref):
    w = w_ref[...].astype(BF16)
    o_ref[...] = r_ref[...] + jnp.dot(a_ref[...], w, preferred_element_type=F32)
    i = pl.program_id(0)

    @pl.when(i == 0)
    def _():
        or_ref[0] = rr_ref[...] + jnp.dot(ar_ref[...], w, preferred_element_type=F32)

    @pl.when(i != 0)
    def _():
        or_ref[...] = jnp.zeros(or_ref.shape, or_ref.dtype)


def _matmul_residual(a, w, res, a_rider, res_rider, *, tm, tn, name):
    m, k = a.shape
    mr = a_rider.shape[0]
    n = _weight_cols(w)
    w, w_spec = _weight_operand(w, k, tn)
    tm = min(tm, m)
    out, out_rider = pl.pallas_call(
        _mm_res_kernel, grid=(m // tm, n // tn),
        in_specs=[pl.BlockSpec((tm, k), lambda i, j: (i, 0)),
                  w_spec,
                  pl.BlockSpec((tm, tn), lambda i, j: (i, j)),
                  pl.BlockSpec((mr, k), lambda i, j: (0, 0)),
                  pl.BlockSpec((mr, tn), lambda i, j: (0, j))],
        out_specs=[pl.BlockSpec((tm, tn), lambda i, j: (i, j)),
                   pl.BlockSpec((1, mr, tn), lambda i, j: (i, 0, j))],
        out_shape=[jax.ShapeDtypeStruct((m, n), F32), jax.ShapeDtypeStruct((m // tm, mr, n), F32)],
        compiler_params=_cparams(("arbitrary", "arbitrary")), name=name,
    )(a, w, res, a_rider, res_rider)
    return out, out_rider[0]


def _rmsnorm_kernel(x_ref, g_ref, o_ref):
    x = x_ref[...]
    ms = jnp.mean(x * x, axis=-1, keepdims=True)
    o_ref[...] = (x * lax.rsqrt(ms + EPS)) * g_ref[...]


def _rmsnorm(x, g, *, tm):
    m, d = x.shape
    tm = min(tm, m)
    return pl.pallas_call(
        _rmsnorm_kernel, grid=(m // tm,),
        in_specs=[pl.BlockSpec((tm, d), lambda i: (i, 0)), pl.BlockSpec((1, d), lambda i: (0, 0))],
        out_specs=pl.BlockSpec((tm, d), lambda i: (i, 0)),
        out_shape=jax.ShapeDtypeStruct((m, d), F32),
        compiler_params=_cparams(("parallel",)), name="final_rmsnorm",
    )(x, g.reshape(1, d))


_EXPM1_SERIES_BOUND = 2.0 ** -6
_EXPM1_SERIES_TERMS = 4


def _neg_expm1_twice(half_x, exp_half_x):
    x = 2.0 * half_x
    poly = jnp.full_like(x, 1.0 / math.factorial(_EXPM1_SERIES_TERMS))
    for k in range(_EXPM1_SERIES_TERMS - 1, 0, -1):
        poly = poly * x + 1.0 / math.factorial(k)
    return jnp.where(x > -_EXPM1_SERIES_BOUND, -(poly * x), 1.0 - exp_half_x * exp_half_x)


def _scan_kernel(u_ref, gate_ref, cw_ref, cb_ref, wa_ref, ba_ref, wi_ref, bi_ref, lam_ref, conv0_ref, h0_ref,
                 hg_ref, convout_ref, hlast_ref, ubuf, a_s, b_s, hs, hcar, *, tc, last_row, n_gate_blocks):
    c = pl.program_id(1)
    width = u_ref.shape[2]
    pad = SUBLANES
    hist = CONV_WIDTH - 1

    @pl.when(c == 0)
    def _():
        ubuf[0:pad, :] = jnp.zeros((pad, width), F32)
        ubuf[pad - hist:pad, :] = conv0_ref[0]
        hcar[...] = jnp.broadcast_to(h0_ref[0], (SUBLANES, width))

    u = u_ref[0]
    ubuf[pad:pad + tc, :] = u
    cw = cw_ref[...]
    xc = cb_ref[...] + u * cw[0:1, :]
    for j in range(1, CONV_WIDTH):
        xc = xc + ubuf[pad - j:pad - j + tc, :] * cw[j:j + 1, :]

    neg_c_softplus = -RG_C * jax.nn.softplus(-lam_ref[...])
    gb = width // n_gate_blocks
    for n in range(n_gate_blocks):
        sl = slice(n * gb, (n + 1) * gb)
        xs = xc[:, sl]
        xb = xs.astype(BF16)
        r = jax.nn.sigmoid(jnp.dot(xb, wa_ref[n].astype(BF16), preferred_element_type=F32) + ba_ref[:, sl])
        i = jax.nn.sigmoid(jnp.dot(xb, wi_ref[n].astype(BF16), preferred_element_type=F32) + bi_ref[:, sl])
        log_a = r * neg_c_softplus[:, sl]
        a = jnp.exp(log_a)
        a_s[:, sl] = a
        b_s[:, sl] = jnp.sqrt(_neg_expm1_twice(log_a, a)) * i * xs

    row = lax.broadcasted_iota(jnp.int32, (SUBLANES, width), 0)

    def group(g, h):
        base = pl.multiple_of(g * SUBLANES, SUBLANES)
        a8 = a_s[pl.ds(base, SUBLANES), :]
        b8 = b_s[pl.ds(base, SUBLANES), :]
        out = jnp.zeros((SUBLANES, width), F32)
        for r in range(SUBLANES):
            hn = a8 * h + b8
            out = jnp.where(row == r, hn, out)
            h = jnp.broadcast_to(hn[r:r + 1, :], (SUBLANES, width))
        hs[pl.ds(base, SUBLANES), :] = out
        return h

    hcar[...] = lax.fori_loop(0, tc // SUBLANES, group, hcar[...])
    hg_ref[0] = (hs[...] * gate_ref[0]).astype(BF16)
    ubuf[0:pad, :] = ubuf[tc:tc + pad, :]

    @pl.when(c == pl.num_programs(1) - 1)
    def _():
        convout_ref[0] = ubuf[pad + last_row + 1 - hist:pad + last_row + 1, :]
        hlast_ref[0] = hs[last_row:last_row + 1, :]


def _conv_rglru(u, gate, conv0, h0, cw, cb, wa, ba, wi, bi, lam, *, tc, t_valid):
    b, t, width = u.shape
    tc = min(tc, t)
    n_chunks = t // tc
    last_row = (t_valid - 1) - (n_chunks - 1) * tc
    nb = wa.shape[0]
    row2 = lambda v: v.reshape(1, width)
    kern = functools.partial(_scan_kernel, tc=tc, last_row=last_row, n_gate_blocks=nb)
    full2 = lambda shape: pl.BlockSpec(shape, lambda i, c: (0, 0))
    return pl.pallas_call(
        kern, grid=(b, n_chunks),
        in_specs=[pl.BlockSpec((1, tc, width), lambda i, c: (i, c, 0)),
                  pl.BlockSpec((1, tc, width), lambda i, c: (i, c, 0)),
                  full2((CONV_WIDTH, width)), full2((1, width)),
                  pl.BlockSpec(wa.shape, lambda i, c: (0, 0, 0)), full2((1, width)),
                  pl.BlockSpec(wi.shape, lambda i, c: (0, 0, 0)), full2((1, width)),
                  full2((1, width)),
                  pl.BlockSpec((1, CONV_WIDTH - 1, width), lambda i, c: (i, 0, 0)),
                  pl.BlockSpec((1, 1, width), lambda i, c: (i, 0, 0))],
        out_specs=[pl.BlockSpec((1, tc, width), lambda i, c: (i, c, 0)),
                   pl.BlockSpec((1, CONV_WIDTH - 1, width), lambda i, c: (i, 0, 0)),
                   pl.BlockSpec((1, 1, width), lambda i, c: (i, 0, 0))],
        out_shape=[jax.ShapeDtypeStruct((b, t, width), BF16),
                   jax.ShapeDtypeStruct((b, CONV_WIDTH - 1, width), F32),
                   jax.ShapeDtypeStruct((b, 1, width), F32)],
        scratch_shapes=[pltpu.VMEM((tc + SUBLANES, width), F32),
                        pltpu.VMEM((tc, width), F32), pltpu.VMEM((tc, width), F32),
                        pltpu.VMEM((tc, width), F32), pltpu.VMEM((SUBLANES, width), F32)],
        compiler_params=_cparams(("parallel", "arbitrary")), name="conv_rglru",
    )(u, gate, cw, row2(cb), wa, row2(ba), wi, row2(bi), row2(lam), conv0, h0.reshape(b, 1, width))


def _diff_lambda(lq1_ref, lk1_ref, lq2_ref, lk2_ref, lam_init):
    e1 = jnp.exp(jnp.sum(lq1_ref[...] * lk1_ref[...], axis=-1, keepdims=True))
    e2 = jnp.exp(jnp.sum(lq2_ref[...] * lk2_ref[...], axis=-1, keepdims=True))
    return e1 - e2 + lam_init


LOG2_E = math.log2(math.e)


def _online_softmax_step(s, v, m_ref, l_ref, acc_ref):
    m_old = m_ref[...]
    m_new = jnp.maximum(m_old, jnp.max(s, axis=-1, keepdims=True))
    alpha = jnp.exp2(m_old - m_new)
    p = jnp.exp2(s - m_new)
    l_ref[...] = alpha * l_ref[...] + jnp.sum(p, axis=-1, keepdims=True)
    acc_ref[...] = alpha * acc_ref[...] + jnp.dot(p.astype(BF16), v, preferred_element_type=F32)
    m_ref[...] = m_new


def _head_out(n0, n1, lam, subln, out_scale):
    o = n0 - lam * n1
    ms = jnp.mean(o * o, axis=-1, keepdims=True)
    return ((o * lax.rsqrt(ms + EPS)) * subln) * out_scale


SUM_ROWS = 16


def _attn_prompt_kernel(qidx_ref, kidx_ref, lq1_ref, lk1_ref, lq2_ref, lk2_ref, subln_ref, bias_ref,
                        qt_ref, k_ref, vt_ref, o_ref, qs_ref, vta_ref, s_ref, cmax_ref, m_ref, acc_ref, *,
                        tq, n_items, lam_init):
    n_q = qt_ref.shape[2] // tq
    feat = lax.broadcasted_iota(jnp.int32, (LANES, tq), 0)
    zero = jnp.zeros((LANES, tq), BF16)
    half = LANES // 2
    for qi in range(n_q):
        qt = qt_ref[0, :, qi * tq:(qi + 1) * tq]
        qs_ref[:, (2 * qi) * tq:(2 * qi + 1) * tq] = jnp.where(feat < half, qt, zero)
        qs_ref[:, (2 * qi + 1) * tq:(2 * qi + 2) * tq] = jnp.where(feat >= half, qt, zero)
    m_ref[...] = jnp.full(m_ref.shape, NEG_BIG, F32)
    acc_ref[...] = jnp.zeros(acc_ref.shape, F32)
    vta_ref[0:LANES, :] = vt_ref[0]
    vta_ref[LANES:, :] = jnp.ones((vta_ref.shape[0] - LANES, vta_ref.shape[1]), BF16)
    lam = _diff_lambda(lq1_ref, lk1_ref, lq2_ref, lk2_ref, lam_init)

    def produce(w, slot):
        qi, ki = qidx_ref[w], kidx_ref[w]
        kstart = pl.multiple_of(ki * tq, tq)
        qstart = pl.multiple_of(qi * (2 * tq), 2 * tq)
        s = jnp.dot(k_ref[0, pl.ds(kstart, tq), :], qs_ref[:, pl.ds(qstart, 2 * tq)],
                    preferred_element_type=F32)
        s = s + bias_ref[(qi == ki).astype(jnp.int32)]
        s_ref[slot] = s
        cmax_ref[slot] = jnp.max(s, axis=0, keepdims=True)

    def consume(w, slot):
        qi, ki = qidx_ref[w], kidx_ref[w]
        kstart = pl.multiple_of(ki * tq, tq)
        m_old = jnp.where(ki == 0, NEG_BIG, m_ref[...])
        m_new = jnp.maximum(m_old, cmax_ref[slot])
        alpha = jnp.exp2(m_old - m_new)
        p = jnp.exp2(s_ref[slot] - m_new)
        pv = jnp.dot(vta_ref[:, pl.ds(kstart, tq)], p.astype(BF16), preferred_element_type=F32)
        acc_ref[qi] = alpha * acc_ref[qi] + pv
        m_ref[...] = m_new

    produce(0, 0)
    per_trip = 4
    n_looped = max(n_items // per_trip - 1, 0)

    def trip(i, carry):
        for g in range(per_trip):
            produce(per_trip * i + g + 1, (g + 1) % 2)
            consume(per_trip * i + g, g % 2)
        return carry

    lax.fori_loop(0, n_looped, trip, 0)
    for w in range(per_trip * n_looped, n_items):
        if w + 1 < n_items:
            produce(w + 1, (w + 1) % 2)
        consume(w, w % 2)

    for qi in range(n_q):
        acc = acc_ref[qi]
        n = acc[0:LANES] * (1.0 / acc[LANES:LANES + 1])
        o = n[:, 0:tq] - lam * n[:, tq:2 * tq]
        ms = jnp.mean(o * o, axis=0, keepdims=True)
        y = ((o * lax.rsqrt(ms + EPS)) * subln_ref[...]) * (1.0 - lam_init)
        o_ref[0, qi * tq:(qi + 1) * tq, :] = y.T.astype(BF16)


def _attn_prompt(qt, k, vt, lq1, lk1, lq2, lk2, subln, lam_init, *, tq):
    b, hw, t = qt.shape
    nh = hw // LANES
    assert t % tq == 0
    n_q = t // tq
    items = [(qi, ki) for qi in range(n_q) for ki in range(qi + 1)]
    qidx = jnp.array([qi for qi, _ in items], jnp.int32)
    kidx = jnp.array([ki for _, ki in items], jnp.int32)
    kpos = lax.broadcasted_iota(jnp.int32, (tq, 2 * tq), 0)
    qpos = lax.broadcasted_iota(jnp.int32, (tq, 2 * tq), 1) % tq
    bias = jnp.stack([jnp.zeros((tq, 2 * tq), F32), jnp.where(kpos <= qpos, 0.0, NEG_BIG).astype(F32)])
    vec = lambda a: a.reshape(1, -1)
    small = lambda w: pl.BlockSpec((1, w), lambda i, h, *_: (0, 0))
    kern = functools.partial(_attn_prompt_kernel, tq=tq, n_items=len(items), lam_init=lam_init)
    grid_spec = pltpu.PrefetchScalarGridSpec(
        num_scalar_prefetch=2, grid=(b, nh),
        in_specs=[small(lq1.size), small(lk1.size), small(lq2.size), small(lk2.size),
                  pl.BlockSpec((LANES, 1), lambda i, h, *_: (0, 0)),
                  pl.BlockSpec((2, tq, 2 * tq), lambda i, h, *_: (0, 0, 0)),
                  pl.BlockSpec((1, LANES, t), lambda i, h, *_: (i, h, 0)),
                  pl.BlockSpec((1, t, LANES), lambda i, h, *_: (i, 0, h)),
                  pl.BlockSpec((1, LANES, t), lambda i, h, *_: (i, h, 0))],
        out_specs=pl.BlockSpec((1, t, LANES), lambda i, h, *_: (i, 0, h)),
        scratch_shapes=[pltpu.VMEM((LANES, 2 * t), BF16), pltpu.VMEM((LANES + SUM_ROWS, t), BF16),
                        pltpu.VMEM((2, tq, 2 * tq), F32), pltpu.VMEM((2, 1, 2 * tq), F32),
                        pltpu.VMEM((1, 2 * tq), F32), pltpu.VMEM((n_q, LANES + SUM_ROWS, 2 * tq), F32)])
    return pl.pallas_call(
        kern, grid_spec=grid_spec, out_shape=jax.ShapeDtypeStruct((b, t, hw), BF16),
        compiler_params=_cparams(("parallel", "parallel")), name="diff_attn_prompt",
    )(qidx, kidx, vec(lq1), vec(lk1), vec(lq2), vec(lk2), subln.reshape(LANES, 1), bias, qt, k, vt)


DEC_ROWS = 2 * SUBLANES


def _attn_decode_kernel(pt_ref, lq1_ref, lk1_ref, lq2_ref, lk2_ref, subln_ref, q_ref, kn_ref, vn_ref, *rest,
                        pages, n_heads, t_new, lam_init):
    k_refs = rest[:pages]
    v_refs = rest[pages:2 * pages]
    o_ref = rest[2 * pages]
    m_ref, l_ref, acc_ref, vh_ref = rest[2 * pages + 1:]
    j = pl.program_id(1)

    @pl.when(j == 0)
    def _():
        m_ref[...] = jnp.full(m_ref.shape, NEG_BIG, F32)
        l_ref[...] = jnp.zeros(l_ref.shape, F32)
        acc_ref[...] = jnp.zeros(acc_ref.shape, F32)

    page = k_refs[0].shape[2]
    for p, vr in enumerate(v_refs):
        v_heads = jnp.swapaxes(vr[0].reshape(page, n_heads, LANES), 0, 1)
        vh_ref[:, p * page:(p + 1) * page, :] = v_heads.astype(BF16)
    head_lanes = lambda h: slice(h * LANES, (h + 1) * LANES)
    head_rows = lambda h: slice(h * DEC_ROWS, (h + 1) * DEC_ROWS)

    def softmax_step(s_heads, v_of_head):
        s = jnp.concatenate(s_heads, axis=0)
        m_old = m_ref[...]
        m_new = jnp.maximum(m_old, jnp.max(s, axis=-1, keepdims=True))
        alpha = jnp.exp2(m_old - m_new)
        p = jnp.exp2(s - m_new)
        l_ref[...] = alpha * l_ref[...] + jnp.sum(p, axis=-1, keepdims=True)
        pb = p.astype(BF16)
        pv = [jnp.dot(pb[head_rows(h)], v_of_head(h), preferred_element_type=F32) for h in range(n_heads)]
        acc_ref[...] = alpha * acc_ref[...] + jnp.concatenate(pv, axis=0)
        m_ref[...] = m_new

    s_heads = [jnp.dot(q_ref[0, h],
                       jnp.concatenate([kr[0, head_lanes(h), :].astype(BF16) for kr in k_refs], axis=1),
                       preferred_element_type=F32) for h in range(n_heads)]
    softmax_step(s_heads, lambda h: vh_ref[h])

    @pl.when(j == pl.num_programs(1) - 1)
    def _():
        lam = _diff_lambda(lq1_ref, lk1_ref, lq2_ref, lk2_ref, lam_init)
        n_new = kn_ref.shape[1]
        qt = lax.broadcasted_iota(jnp.int32, (DEC_ROWS, n_new), 0) % SUBLANES
        kt_pos = lax.broadcasted_iota(jnp.int32, (DEC_ROWS, n_new), 1)
        visible = (kt_pos <= qt) & (kt_pos < t_new)
        s_new = [jnp.where(visible,
                           lax.dot_general(q_ref[0, h], kn_ref[0, :, head_lanes(h)], (((1,), (1,)), ((), ())),
                                           preferred_element_type=F32), NEG_BIG) for h in range(n_heads)]
        softmax_step(s_new, lambda h: vn_ref[0, :, head_lanes(h)])
        n = acc_ref[...] / l_ref[...]
        for h in range(n_heads):
            nh_ = n[head_rows(h)]
            o_ref[0, :, head_lanes(h)] = _head_out(nh_[0:SUBLANES], nh_[SUBLANES:DEC_ROWS], lam, subln_ref[...],
                                                   1.0 - lam_init)


def _attn_decode(qd, k_new, v_new, cache_k, cache_v, page_table, lq1, lk1, lq2, lk2, subln, lam_init, *,
                 pages, t_new):
    b, nh = qd.shape[0], qd.shape[1]
    hw = nh * LANES
    n_pages = page_table.shape[1]
    assert n_pages % pages == 0
    vec = lambda a: a.reshape(1, -1)
    small = lambda w: pl.BlockSpec((1, w), lambda i, j, pt: (0, 0))

    def page_spec(cache, p):
        return pl.BlockSpec((1,) + cache.shape[1:], lambda i, j, pt: (pt[i, j * pages + p], 0, 0))

    kern = functools.partial(_attn_decode_kernel, pages=pages, n_heads=nh, t_new=t_new, lam_init=lam_init)
    grid_spec = pltpu.PrefetchScalarGridSpec(
        num_scalar_prefetch=1, grid=(b, n_pages // pages),
        in_specs=[small(lq1.size), small(lk1.size), small(lq2.size), small(lk2.size), small(LANES),
                  pl.BlockSpec((1, nh, DEC_ROWS, LANES), lambda i, j, pt: (i, 0, 0, 0)),
                  pl.BlockSpec((1, k_new.shape[1], hw), lambda i, j, pt: (i, 0, 0)),
                  pl.BlockSpec((1, v_new.shape[1], hw), lambda i, j, pt: (i, 0, 0))]
                 + [page_spec(cache_k, p) for p in range(pages)] + [page_spec(cache_v, p) for p in range(pages)],
        out_specs=pl.BlockSpec((1, SUBLANES, hw), lambda i, j, pt: (i, 0, 0)),
        scratch_shapes=[pltpu.VMEM((nh * DEC_ROWS, 1), F32), pltpu.VMEM((nh * DEC_ROWS, 1), F32),
                        pltpu.VMEM((nh * DEC_ROWS, LANES), F32),
                        pltpu.VMEM((nh, pages * cache_k.shape[2], LANES), BF16)])
    return pl.pallas_call(
        kern, grid_spec=grid_spec, out_shape=jax.ShapeDtypeStruct((b, SUBLANES, hw), F32),
        compiler_params=_cparams(("parallel", "arbitrary")), name="diff_attn_decode",
    )(page_table, vec(lq1), vec(lk1), vec(lq2), vec(lk2), vec(subln), qd, k_new, v_new,
      *([cache_k] * pages), *([cache_v] * pages))


def _rope_tables(pos, head_dim, batch):
    half = head_dim // 2
    inv = ROPE_THETA ** (-jnp.arange(half, dtype=F32) / half)
    ang = pos.astype(F32)[:, None] * inv[None, :]
    cos, sin = jnp.cos(ang), jnp.sin(ang)
    reps = LANES // head_dim
    cos_t = jnp.tile(jnp.concatenate([cos, cos], axis=1), (batch, reps))
    sin_t = jnp.tile(jnp.concatenate([-sin, sin], axis=1), (batch, reps))
    return cos_t, sin_t


def _plain(*dtypes):
    return [(dt, False) for dt in dtypes]


class _Group:
    def __init__(self, x, pos, conv0, h0, decode, cfg, head_dim):
        self.b, self.t, d = x.shape
        self.m = self.b * self.t
        self.x = x.reshape(self.m, d)
        self.cos, self.sin = _rope_tables(pos, head_dim, self.b)
        self.conv0, self.h0, self.decode, self.cfg = conv0, h0, decode, cfg
        self.prompt = decode is None
        self.new_bufs, self.new_hs = [], []
        self.kb = self.vb = self.k_out = self.v_new = None

    def mixer_recurrent(self, gate, u, l, p):
        b, t = self.b, self.t
        width = u.shape[1]
        t_pad = -(-t // SUBLANES) * SUBLANES
        u3, g3 = u.reshape(b, t, width), gate.reshape(b, t, width)
        if t_pad != t:
            u3 = jnp.pad(u3, ((0, 0), (0, t_pad - t), (0, 0)))
            g3 = jnp.pad(g3, ((0, 0), (0, t_pad - t), (0, 0)))
        hg, nb, nh = _conv_rglru(u3, g3, self.conv0[l], self.h0[l], p['rg_conv_w'][l], p['rg_conv_b'][l],
                                 p['rg_w_a'][l], p['rg_b_a'][l], p['rg_w_i'][l], p['rg_b_i'][l],
                                 p['rg_lambda'][l], tc=self.cfg['tc'], t_valid=t)
        self.new_bufs.append(nb)
        self.new_hs.append(nh.reshape(b, width))
        return hg[:, :t].reshape(self.m, width)

    def shared_kv(self, p, n_heads, head_dim):
        b, t, cfg = self.b, self.t, self.cfg
        qk_width = n_heads * 2 * head_dim
        assert p['w_kv'].shape[1] == 2 * qk_width
        kv_ws = dict(ws=[p['w_kv'], p['w_kv']], n=qk_width, col_starts=[0, qk_width])
        common = dict(extras=[self.cos, self.sin], tm=cfg['tm'], tn=cfg['tn'], name="shared_kv", **kv_ws)
        if self.prompt:
            k_t, self.kb, self.v_new, self.vb = _norm_matmul(
                self.x, p['kv_norm'], outs=[(F32, True), (BF16, False), (F32, False), (BF16, True)],
                epi=_epi_kv_transposed, seq_len=t, **common)
            self.k_out = k_t.reshape(b, n_heads, 2, head_dim, t).transpose(0, 4, 1, 2, 3)
        else:
            k_new, self.kb, self.v_new, self.vb = _norm_matmul(
                self.x, p['kv_norm'], outs=_plain(F32, BF16, F32, BF16), epi=_epi_kv, **common)
            self.k_out = k_new.reshape(b, t, n_heads, 2, head_dim)

    def mixer_attention(self, l, j, p, n_heads, head_dim):
        b, t, m, cfg = self.b, self.t, self.m, self.cfg
        lam_init = 0.8 - 0.6 * math.exp(-0.3 * l)
        (q,) = _norm_matmul(self.x, p['norm_mix'][l], [(p['dif_w_q'], j)], [self.cos, self.sin],
                            [(BF16, self.prompt)],
                            functools.partial(_epi_q, scale=head_dim ** -0.5 * LOG2_E, transposed=self.prompt),
                            tm=cfg['tm'], tn=cfg['tn'], name="q_proj", seq_len=t)
        lams = (p['dif_lq1'][j], p['dif_lk1'][j], p['dif_lq2'][j], p['dif_lk2'][j], p['dif_subln'][j])
        if self.prompt:
            o = _attn_prompt(q, self.kb.reshape(b, t, -1), self.vb, *lams, lam_init, tq=cfg['tq'])
            return o.reshape(m, -1)
        cache_k, cache_v, page_table = self.decode
        q4 = q.reshape(b, t, n_heads, LANES).transpose(0, 2, 1, 3)
        lane = jnp.arange(LANES)
        zpad = jnp.zeros((b, n_heads, SUBLANES - t, LANES), BF16)
        qd = jnp.concatenate([jnp.where(lane < head_dim, q4, 0).astype(BF16), zpad,
                              jnp.where(lane >= head_dim, q4, 0).astype(BF16), zpad], axis=2)
        pad_new = lambda a: jnp.pad(a.reshape(b, t, -1), ((0, 0), (0, DEC_ROWS - t), (0, 0)))
        o = _attn_decode(qd, pad_new(self.kb), pad_new(self.vb), cache_k, cache_v, page_table, *lams, lam_init,
                         pages=cfg['pages'], t_new=t)
        return o[:, :t].reshape(m, -1).astype(BF16)

    def outputs(self, p, n_heads):
        d = self.x.shape[1]
        y = _rmsnorm(self.x, p['norm_final'], tm=self.cfg['tm_norm']).reshape(self.b, self.t, d)
        v_out = self.v_new.reshape(self.b, self.t, n_heads, -1)
        return y, self.k_out, v_out, jnp.stack(self.new_bufs), jnp.stack(self.new_hs)


def _trunks(main, rider, p):
    depth = p['norm_mix'].shape[0]
    n_a = p['rg_w_x'].shape[0]
    head_dim = p['dif_lq1'].shape[1]
    n_heads = p['dif_w_q'].shape[2] // (2 * head_dim)
    cfg = main.cfg
    tm, tn = cfg['tm'], cfg['tn']
    for l in range(depth):
        if l < n_a:
            gate, u, gate_r, u_r = _norm_matmul(
                main.x, p['norm_mix'][l], [(p['rg_w_gate'], l), (p['rg_w_x'], l)], [], _plain(F32, F32),
                _epi_recurrent_in, tm=tm, tn=tn, name="recurrent_in", rider=rider.x)
            mixed, mixed_r = main.mixer_recurrent(gate, u, l, p), rider.mixer_recurrent(gate_r, u_r, l, p)
            w_out = (p['rg_w_out'], l)
        else:
            j = l - n_a
            if j == 0:
                main.shared_kv(p, n_heads, head_dim)
                rider.shared_kv(p, n_heads, head_dim)
            mixed = main.mixer_attention(l, j, p, n_heads, head_dim)
            mixed_r = rider.mixer_attention(l, j, p, n_heads, head_dim)
            w_out = (p['dif_w_o'], j)
        main.x, rider.x = _matmul_residual(mixed, w_out, main.x, mixed_r, rider.x, tm=cfg['tm_out'], tn=tn,
                                           name="mixer_out")
        hmid, hmid_r = _norm_matmul(main.x, p['norm_ffn'][l], [(p['ffn_w_gate'], l), (p['ffn_w_up'], l)], [],
                                    _plain(BF16), _epi_swiglu, tm=tm, tn=tn, name="ffn_in", rider=rider.x)
        main.x, rider.x = _matmul_residual(hmid, (p['ffn_w_down'], l), main.x, hmid_r, rider.x,
                                           tm=cfg['tm_down'], tn=tn, name="ffn_down")
    return main.outputs(p, n_heads), rider.outputs(p, n_heads)


def kernel(x_prompt, x_sample, cache_k, cache_v, page_table, state_conv, state_rglru, norm_mix, norm_ffn, norm_final, rg_w_x, rg_w_gate, rg_conv_w, rg_conv_b, rg_w_a, rg_b_a, rg_w_i, rg_b_i, rg_lambda, rg_w_out, kv_norm, w_kv, dif_w_q, dif_lq1, dif_lk1, dif_lq2, dif_lk2, dif_subln, dif_w_o, ffn_w_gate, ffn_w_up, ffn_w_down):
    p = dict(norm_mix=norm_mix, norm_ffn=norm_ffn, norm_final=norm_final, rg_w_x=rg_w_x,
             rg_w_gate=rg_w_gate, rg_conv_w=rg_conv_w, rg_conv_b=rg_conv_b, rg_w_a=rg_w_a,
             rg_b_a=rg_b_a, rg_w_i=rg_w_i, rg_b_i=rg_b_i, rg_lambda=rg_lambda, rg_w_out=rg_w_out,
             kv_norm=kv_norm, w_kv=w_kv, dif_w_q=dif_w_q, dif_lq1=dif_lq1, dif_lk1=dif_lk1,
             dif_lq2=dif_lq2, dif_lk2=dif_lk2, dif_subln=dif_subln, dif_w_o=dif_w_o,
             ffn_w_gate=ffn_w_gate, ffn_w_up=ffn_w_up, ffn_w_down=ffn_w_down)
    p['ffn_w_down'] = ffn_w_down.astype(BF16)

    n_a = rg_w_x.shape[0]
    width = rg_w_x.shape[2]
    head_dim = dif_lq1.shape[1]
    b_p, t_p = x_prompt.shape[0], x_prompt.shape[1]
    cfg_p = dict(tm=1024, tn=512, tm_out=2048, tm_down=1024, tm_norm=512, tc=256, tq=512)
    conv0 = jnp.zeros((n_a, b_p, CONV_WIDTH - 1, width), F32)
    h0 = jnp.zeros((n_a, b_p, width), F32)
    prompt = _Group(x_prompt, jnp.arange(t_p, dtype=jnp.int32), conv0, h0, None, cfg_p, head_dim)

    b_s, t_s = x_sample.shape[0], x_sample.shape[1]
    n_phys, page = cache_k.shape[0], cache_k.shape[1]
    past_len = page_table.shape[1] * page
    cfg_s = dict(tm=b_s * t_s, tn=512, tm_norm=b_s * t_s, tc=SUBLANES, pages=8)
    k_pages = jnp.transpose(cache_k, (0, 2, 3, 4, 1)).reshape(n_phys, -1, page)
    v_pages = cache_v.reshape(n_phys, page * cache_v.shape[2], cache_v.shape[3])
    pos_s = past_len + jnp.arange(t_s, dtype=jnp.int32)
    sample = _Group(x_sample, pos_s, state_conv, state_rglru, (k_pages, v_pages, page_table), cfg_s, head_dim)

    (y_p, k_p, v_p, conv_p, h_p), (y_s, k_s, v_s, conv_s, h_s) = _trunks(prompt, sample, p)
    return (y_p, y_s, k_p, v_p, conv_p, h_p, k_s, v_s, conv_s, h_s)
```

```python
import functools
import math

import jax
import jax.numpy as jnp
from jax import lax
from jax.experimental import pallas as pl
from jax.experimental.pallas import tpu as pltpu

F32 = jnp.float32
BF16 = jnp.bfloat16

EPS = 1e-6
RG_C = 8.0
ROPE_THETA = 10000.0
CONV_WIDTH = 4
LANES = 128
SUBLANES = 8
NEG_BIG = -1e30
VMEM_LIMIT = 56 * 1024 * 1024


def _cparams(sem):
    return pltpu.CompilerParams(dimension_semantics=sem, vmem_limit_bytes=VMEM_LIMIT)


def _weight_operand(w, rows, tn, first_block=0):
    if isinstance(w, tuple):
        stacked, layer = w
        return stacked, pl.BlockSpec((None, rows, tn), lambda i, j: (layer, 0, first_block + j))
    return w, pl.BlockSpec((rows, tn), lambda i, j: (0, first_block + j))


def _weight_cols(w):
    return w[0].shape[2] if isinstance(w, tuple) else w.shape[1]


def _rmsnorm_bf16(x, g):
    ms = jnp.mean(x * x, axis=-1, keepdims=True)
    return ((x * lax.rsqrt(ms + EPS)) * g).astype(BF16)


def _nmm_kernel(*refs, n_w, n_extra, n_out, epi, has_rider):
    x_ref, g_ref = refs[0], refs[1]
    w_refs = refs[2:2 + n_w]
    e_refs = refs[2 + n_w:2 + n_w + n_extra]
    pos = 2 + n_w + n_extra
    if has_rider:
        xr_ref = refs[pos]
        pos += 1
    o_refs = refs[pos:pos + n_out]
    i, j = pl.program_id(0), pl.program_id(1)

    @pl.when(j == 0)
    def _():
        refs[-1][...] = _rmsnorm_bf16(x_ref[...], g_ref[...])

    xn = refs[-1][...]
    w_tiles = [w[...].astype(BF16) for w in w_refs]
    epi([jnp.dot(xn, w, preferred_element_type=F32) for w in w_tiles], e_refs, o_refs)

    if has_rider:
        ro_refs = refs[pos + n_out:pos + 2 * n_out]
        xrn_ref = refs[-2]

        @pl.when(i == 0)
        def _():
            @pl.when(j == 0)
            def _():
                xrn_ref[...] = _rmsnorm_bf16(xr_ref[...], g_ref[...])

            xrn = xrn_ref[...]
            epi([jnp.dot(xrn, w, preferred_element_type=F32) for w in w_tiles], e_refs, [o.at[0] for o in ro_refs])

        @pl.when(i != 0)
        def _():
            for o in ro_refs:
                o[...] = jnp.zeros(o.shape, o.dtype)


def _norm_matmul(x, g, ws, extras, outs, epi, *, tm, tn, name, seq_len=None, n=None, col_starts=None,
                 rider=None):
    m, d = x.shape
    n = _weight_cols(ws[0]) if n is None else n
    col_starts = [0] * len(ws) if col_starts is None else col_starts
    tm = min(tm, m)
    grid = (m // tm, n // tn)
    in_specs = [pl.BlockSpec((tm, d), lambda i, j: (i, 0)),
                pl.BlockSpec((1, d), lambda i, j: (0, 0))]
    w_arrays = []
    for w, c in zip(ws, col_starts):
        arr, spec = _weight_operand(w, d, tn, c // tn)
        w_arrays.append(arr)
        in_specs.append(spec)
    in_specs += [pl.BlockSpec((tm, LANES), lambda i, j: (i, 0)) for _ in extras]
    out_specs, out_shape = [], []
    for dt, transposed in outs:
        if transposed:
            per_seq = seq_len // tm
            out_specs.append(pl.BlockSpec((1, tn, tm), lambda i, j: (i // per_seq, j, i % per_seq)))
            out_shape.append(jax.ShapeDtypeStruct((m // seq_len, n, seq_len), dt))
        else:
            out_specs.append(pl.BlockSpec((tm, tn), lambda i, j: (i, j)))
            out_shape.append(jax.ShapeDtypeStruct((m, n), dt))
    scratch = [pltpu.VMEM((tm, d), BF16)]
    operands = [x, g.reshape(1, d), *w_arrays, *extras]
    if rider is not None:
        assert not extras and not any(transposed for _, transposed in outs)
        mr = rider.shape[0]
        in_specs.append(pl.BlockSpec((mr, d), lambda i, j: (0, 0)))
        operands.append(rider)
        for dt, _ in outs:
            out_specs.append(pl.BlockSpec((1, mr, tn), lambda i, j: (i, 0, j)))
            out_shape.append(jax.ShapeDtypeStruct((grid[0], mr, n), dt))
        scratch.insert(0, pltpu.VMEM((mr, d), BF16))
    kern = functools.partial(_nmm_kernel, n_w=len(ws), n_extra=len(extras), n_out=len(outs), epi=epi,
                             has_rider=rider is not None)
    res = pl.pallas_call(
        kern, grid=grid, in_specs=in_specs, out_specs=out_specs, out_shape=out_shape,
        scratch_shapes=scratch, compiler_params=_cparams(("arbitrary", "arbitrary")), name=name,
    )(*operands)
    if rider is not None:
        res = list(res[:len(outs)]) + [r[0] for r in res[len(outs):]]
    return res


def _epi_recurrent_in(accs, e_refs, o_refs):
    o_refs[0][...] = jax.nn.gelu(accs[0])
    o_refs[1][...] = accs[1]


def _epi_swiglu(accs, e_refs, o_refs):
    o_refs[0][...] = (jax.nn.silu(accs[0]) * accs[1]).astype(BF16)


def _rope_slab(y, cos, sin_signed):
    lane = lax.broadcasted_iota(jnp.int32, y.shape, 1)
    first_half = (lane % 64) < 32
    rot = jnp.where(first_half, pltpu.roll(y, LANES - 32, 1), pltpu.roll(y, 32, 1))
    return y * cos + rot * sin_signed


def _epi_q(accs, e_refs, o_refs, *, scale, transposed):
    cos, sin_signed = e_refs[0][...], e_refs[1][...]
    for s in range(accs[0].shape[1] // LANES):
        sl = slice(s * LANES, (s + 1) * LANES)
        q = _rope_slab(accs[0][:, sl], cos, sin_signed) * scale
        if transposed:
            o_refs[0][0, sl, :] = q.T.astype(BF16)
        else:
            o_refs[0][:, sl] = q.astype(BF16)


def _epi_kv(accs, e_refs, o_refs):
    cos, sin_signed = e_refs[0][...], e_refs[1][...]
    for s in range(accs[0].shape[1] // LANES):
        sl = slice(s * LANES, (s + 1) * LANES)
        k = _rope_slab(accs[0][:, sl], cos, sin_signed)
        o_refs[0][:, sl] = k
        o_refs[1][:, sl] = k.astype(BF16)
    o_refs[2][...] = accs[1]
    o_refs[3][...] = accs[1].astype(BF16)


def _epi_kv_transposed(accs, e_refs, o_refs):
    cos, sin_signed = e_refs[0][...], e_refs[1][...]
    for s in range(accs[0].shape[1] // LANES):
        sl = slice(s * LANES, (s + 1) * LANES)
        k = _rope_slab(accs[0][:, sl], cos, sin_signed)
        o_refs[0][0, sl, :] = k.T
        o_refs[1][:, sl] = k.astype(BF16)
        o_refs[3][0, sl, :] = accs[1][:, sl].T.astype(BF16)
    o_refs[2][...] = accs[1]


def _mm_res_kernel(a_ref, w_ref, r_ref, ar_ref, rr_ref, o_ref, or_ref):
    w = w_ref[...].astype(BF16)
    o_ref[...] = r_ref[...] + jnp.dot(a_ref[...], w, preferred_element_type=F32)
    i = pl.program_id(0)

    @pl.when(i == 0)
    def _():
        or_ref[0] = rr_ref[...] + jnp.dot(ar_ref[...], w, preferred_element_type=F32)

    @pl.when(i != 0)
    def _():
        or_ref[...] = jnp.zeros(or_ref.shape, or_ref.dtype)


def _matmul_residual(a, w, res, a_rider, res_rider, *, tm, tn, name):
    m, k = a.shape
    mr = a_rider.shape[0]
    n = _weight_cols(w)
    w, w_spec = _weight_operand(w, k, tn)
    tm = min(tm, m)
    out, out_rider = pl.pallas_call(
        _mm_res_kernel, grid=(m // tm, n // tn),
        in_specs=[pl.BlockSpec((tm, k), lambda i, j: (i, 0)),
                  w_spec,
                  pl.BlockSpec((tm, tn), lambda i, j: (i, j)),
                  pl.BlockSpec((mr, k), lambda i, j: (0, 0)),
                  pl.BlockSpec((mr, tn), lambda i, j: (0, j))],
        out_specs=[pl.BlockSpec((tm, tn), lambda i, j: (i, j)),
                   pl.BlockSpec((1, mr, tn), lambda i, j: (i, 0, j))],
        out_shape=[jax.ShapeDtypeStruct((m, n), F32), jax.ShapeDtypeStruct((m // tm, mr, n), F32)],
        compiler_params=_cparams(("arbitrary", "arbitrary")), name=name,
    )(a, w, res, a_rider, res_rider)
    return out, out_rider[0]


def _rmsnorm_kernel(x_ref, g_ref, o_ref):
    x = x_ref[...]
    ms = jnp.mean(x * x, axis=-1, keepdims=True)
    o_ref[...] = (x * lax.rsqrt(ms + EPS)) * g_ref[...]


def _rmsnorm(x, g, *, tm):
    m, d = x.shape
    tm = min(tm, m)
    return pl.pallas_call(
        _rmsnorm_kernel, grid=(m // tm,),
        in_specs=[pl.BlockSpec((tm, d), lambda i: (i, 0)), pl.BlockSpec((1, d), lambda i: (0, 0))],
        out_specs=pl.BlockSpec((tm, d), lambda i: (i, 0)),
        out_shape=jax.ShapeDtypeStruct((m, d), F32),
        compiler_params=_cparams(("parallel",)), name="final_rmsnorm",
    )(x, g.reshape(1, d))


_EXPM1_SERIES_BOUND = 2.0 ** -6
_EXPM1_SERIES_TERMS = 4


def _neg_expm1_twice(half_x, exp_half_x):
    x = 2.0 * half_x
    poly = jnp.full_like(x, 1.0 / math.factorial(_EXPM1_SERIES_TERMS))
    for k in range(_EXPM1_SERIES_TERMS - 1, 0, -1):
        poly = poly * x + 1.0 / math.factorial(k)
    return jnp.where(x > -_EXPM1_SERIES_BOUND, -(poly * x), 1.0 - exp_half_x * exp_half_x)


def _scan_kernel(u_ref, gate_ref, cw_ref, cb_ref, wa_ref, ba_ref, wi_ref, bi_ref, lam_ref, conv0_ref, h0_ref,
                 hg_ref, convout_ref, hlast_ref, ubuf, a_s, b_s, hs, hcar, *, tc, last_row, n_gate_blocks):
    c = pl.program_id(1)
    width = u_ref.shape[2]
    pad = SUBLANES
    hist = CONV_WIDTH - 1

    @pl.when(c == 0)
    def _():
        ubuf[0:pad, :] = jnp.zeros((pad, width), F32)
        ubuf[pad - hist:pad, :] = conv0_ref[0]
        hcar[...] = jnp.broadcast_to(h0_ref[0], (SUBLANES, width))

    u = u_ref[0]
    ubuf[pad:pad + tc, :] = u
    cw = cw_ref[...]
    xc = cb_ref[...] + u * cw[0:1, :]
    for j in range(1, CONV_WIDTH):
        xc = xc + ubuf[pad - j:pad - j + tc, :] * cw[j:j + 1, :]

    neg_c_softplus = -RG_C * jax.nn.softplus(-lam_ref[...])
    gb = width // n_gate_blocks
    for n in range(n_gate_blocks):
        sl = slice(n * gb, (n + 1) * gb)
        xs = xc[:, sl]
        xb = xs.astype(BF16)
        r = jax.nn.sigmoid(jnp.dot(xb, wa_ref[n].astype(BF16), preferred_element_type=F32) + ba_ref[:, sl])
        i = jax.nn.sigmoid(jnp.dot(xb, wi_ref[n].astype(BF16), preferred_element_type=F32) + bi_ref[:, sl])
        log_a = r * neg_c_softplus[:, sl]
        a = jnp.exp(log_a)
        a_s[:, sl] = a
        b_s[:, sl] = jnp.sqrt(_neg_expm1_twice(log_a, a)) * i * xs

    row = lax.broadcasted_iota(jnp.int32, (SUBLANES, width), 0)

    def group(g, h):
        base = pl.multiple_of(g * SUBLANES, SUBLANES)
        a8 = a_s[pl.ds(base, SUBLANES), :]
        b8 = b_s[pl.ds(base, SUBLANES), :]
        out = jnp.zeros((SUBLANES, width), F32)
        for r in range(SUBLANES):
            hn = a8 * h + b8
            out = jnp.where(row == r, hn, out)
            h = jnp.broadcast_to(hn[r:r + 1, :], (SUBLANES, width))
        hs[pl.ds(base, SUBLANES), :] = out
        return h

    hcar[...] = lax.fori_loop(0, tc // SUBLANES, group, hcar[...])
    hg_ref[0] = (hs[...] * gate_ref[0]).astype(BF16)
    ubuf[0:pad, :] = ubuf[tc:tc + pad, :]

    @pl.when(c == pl.num_programs(1) - 1)
    def _():
        convout_ref[0] = ubuf[pad + last_row + 1 - hist:pad + last_row + 1, :]
        hlast_ref[0] = hs[last_row:last_row + 1, :]


def _conv_rglru(u, gate, conv0, h0, cw, cb, wa, ba, wi, bi, lam, *, tc, t_valid):
    b, t, width = u.shape
    tc = min(tc, t)
    n_chunks = t // tc
    last_row = (t_valid - 1) - (n_chunks - 1) * tc
    nb = wa.shape[0]
    row2 = lambda v: v.reshape(1, width)
    kern = functools.partial(_scan_kernel, tc=tc, last_row=last_row, n_gate_blocks=nb)
    full2 = lambda shape: pl.BlockSpec(shape, lambda i, c: (0, 0))
    return pl.pallas_call(
        kern, grid=(b, n_chunks),
        in_specs=[pl.BlockSpec((1, tc, width), lambda i, c: (i, c, 0)),
                  pl.BlockSpec((1, tc, width), lambda i, c: (i, c, 0)),
                  full2((CONV_WIDTH, width)), full2((1, width)),
                  pl.BlockSpec(wa.shape, lambda i, c: (0, 0, 0)), full2((1, width)),
                  pl.BlockSpec(wi.shape, lambda i, c: (0, 0, 0)), full2((1, width)),
                  full2((1, width)),
                  pl.BlockSpec((1, CONV_WIDTH - 1, width), lambda i, c: (i, 0, 0)),
                  pl.BlockSpec((1, 1, width), lambda i, c: (i, 0, 0))],
        out_specs=[pl.BlockSpec((1, tc, width), lambda i, c: (i, c, 0)),
                   pl.BlockSpec((1, CONV_WIDTH - 1, width), lambda i, c: (i, 0, 0)),
                   pl.BlockSpec((1, 1, width), lambda i, c: (i, 0, 0))],
        out_shape=[jax.ShapeDtypeStruct((b, t, width), BF16),
                   jax.ShapeDtypeStruct((b, CONV_WIDTH - 1, width), F32),
                   jax.ShapeDtypeStruct((b, 1, width), F32)],
        scratch_shapes=[pltpu.VMEM((tc + SUBLANES, width), F32),
                        pltpu.VMEM((tc, width), F32), pltpu.VMEM((tc, width), F32),
                        pltpu.VMEM((tc, width), F32), pltpu.VMEM((SUBLANES, width), F32)],
        compiler_params=_cparams(("parallel", "arbitrary")), name="conv_rglru",
    )(u, gate, cw, row2(cb), wa, row2(ba), wi, row2(bi), row2(lam), conv0, h0.reshape(b, 1, width))


def _diff_lambda(lq1_ref, lk1_ref, lq2_ref, lk2_ref, lam_init):
    e1 = jnp.exp(jnp.sum(lq1_ref[...] * lk1_ref[...], axis=-1, keepdims=True))
    e2 = jnp.exp(jnp.sum(lq2_ref[...] * lk2_ref[...], axis=-1, keepdims=True))
    return e1 - e2 + lam_init


LOG2_E = math.log2(math.e)


def _online_softmax_step(s, v, m_ref, l_ref, acc_ref):
    m_old = m_ref[...]
    m_new = jnp.maximum(m_old, jnp.max(s, axis=-1, keepdims=True))
    alpha = jnp.exp2(m_old - m_new)
    p = jnp.exp2(s - m_new)
    l_ref[...] = alpha * l_ref[...] + jnp.sum(p, axis=-1, keepdims=True)
    acc_ref[...] = alpha * acc_ref[...] + jnp.dot(p.astype(BF16), v, preferred_element_type=F32)
    m_ref[...] = m_new


def _head_out(n0, n1, lam, subln, out_scale):
    o = n0 - lam * n1
    ms = jnp.mean(o * o, axis=-1, keepdims=True)
    return ((o * lax.rsqrt(ms + EPS)) * subln) * out_scale


SUM_ROWS = 16


def _attn_prompt_kernel(qidx_ref, kidx_ref, lq1_ref, lk1_ref, lq2_ref, lk2_ref, subln_ref, bias_ref,
                        qt_ref, k_ref, vt_ref, o_ref, qs_ref, vta_ref, s_ref, cmax_ref, m_ref, acc_ref, *,
                        tq, n_items, lam_init):
    n_q = qt_ref.shape[2] // tq
    feat = lax.broadcasted_iota(jnp.int32, (LANES, tq), 0)
    zero = jnp.zeros((LANES, tq), BF16)
    half = LANES // 2
    for qi in range(n_q):
        qt = qt_ref[0, :, qi * tq:(qi + 1) * tq]
        qs_ref[:, (2 * qi) * tq:(2 * qi + 1) * tq] = jnp.where(feat < half, qt, zero)
        qs_ref[:, (2 * qi + 1) * tq:(2 * qi + 2) * tq] = jnp.where(feat >= half, qt, zero)
    m_ref[...] = jnp.full(m_ref.shape, NEG_BIG, F32)
    acc_ref[...] = jnp.zeros(acc_ref.shape, F32)
    vta_ref[0:LANES, :] = vt_ref[0]
    vta_ref[LANES:, :] = jnp.ones((vta_ref.shape[0] - LANES, vta_ref.shape[1]), BF16)
    lam = _diff_lambda(lq1_ref, lk1_ref, lq2_ref, lk2_ref, lam_init)

    def produce(w, slot):
        qi, ki = qidx_ref[w], kidx_ref[w]
        kstart = pl.multiple_of(ki * tq, tq)
        qstart = pl.multiple_of(qi * (2 * tq), 2 * tq)
        s = jnp.dot(k_ref[0, pl.ds(kstart, tq), :], qs_ref[:, pl.ds(qstart, 2 * tq)],
                    preferred_element_type=F32)
        s = s + bias_ref[(qi == ki).astype(jnp.int32)]
        s_ref[slot] = s
        cmax_ref[slot] = jnp.max(s, axis=0, keepdims=True)

    def consume(w, slot):
        qi, ki = qidx_ref[w], kidx_ref[w]
        kstart = pl.multiple_of(ki * tq, tq)
        m_old = jnp.where(ki == 0, NEG_BIG, m_ref[...])
        m_new = jnp.maximum(m_old, cmax_ref[slot])
        alpha = jnp.exp2(m_old - m_new)
        p = jnp.exp2(s_ref[slot] - m_new)
        pv = jnp.dot(vta_ref[:, pl.ds(kstart, tq)], p.astype(BF16), preferred_element_type=F32)
        acc_ref[qi] = alpha * acc_ref[qi] + pv
        m_ref[...] = m_new

    produce(0, 0)
    per_trip = 6
    n_looped = max(n_items // per_trip - 1, 0)

    def trip(i, carry):
        for g in range(per_trip):
            produce(per_trip * i + g + 1, (g + 1) % 2)
            consume(per_trip * i + g, g % 2)
        return carry

    lax.fori_loop(0, n_looped, trip, 0)
    for w in range(per_trip * n_looped, n_items):
        if w + 1 < n_items:
            produce(w + 1, (w + 1) % 2)
        consume(w, w % 2)

    for qi in range(n_q):
        acc = acc_ref[qi]
        n = acc[0:LANES] * (1.0 / acc[LANES:LANES + 1])
        o = n[:, 0:tq] - lam * n[:, tq:2 * tq]
        ms = jnp.mean(o * o, axis=0, keepdims=True)
        y = ((o * lax.rsqrt(ms + EPS)) * subln_ref[...]) * (1.0 - lam_init)
        o_ref[0, qi * tq:(qi + 1) * tq, :] = y.T.astype(BF16)


def _attn_prompt(qt, k, vt, lq1, lk1, lq2, lk2, subln, lam_init, *, tq):
    b, hw, t = qt.shape
    nh = hw // LANES
    assert t % tq == 0
    n_q = t // tq
    items = [(qi, ki) for qi in range(n_q) for ki in range(qi + 1)]
    qidx = jnp.array([qi for qi, _ in items], jnp.int32)
    kidx = jnp.array([ki for _, ki in items], jnp.int32)
    kpos = lax.broadcasted_iota(jnp.int32, (tq, 2 * tq), 0)
    qpos = lax.broadcasted_iota(jnp.int32, (tq, 2 * tq), 1) % tq
    bias = jnp.stack([jnp.zeros((tq, 2 * tq), F32), jnp.where(kpos <= qpos, 0.0, NEG_BIG).astype(F32)])
    vec = lambda a: a.reshape(1, -1)
    small = lambda w: pl.BlockSpec((1, w), lambda i, h, *_: (0, 0))
    kern = functools.partial(_attn_prompt_kernel, tq=tq, n_items=len(items), lam_init=lam_init)
    grid_spec = pltpu.PrefetchScalarGridSpec(
        num_scalar_prefetch=2, grid=(b, nh),
        in_specs=[small(lq1.size), small(lk1.size), small(lq2.size), small(lk2.size),
                  pl.BlockSpec((LANES, 1), lambda i, h, *_: (0, 0)),
                  pl.BlockSpec((2, tq, 2 * tq), lambda i, h, *_: (0, 0, 0)),
                  pl.BlockSpec((1, LANES, t), lambda i, h, *_: (i, h, 0)),
                  pl.BlockSpec((1, t, LANES), lambda i, h, *_: (i, 0, h)),
                  pl.BlockSpec((1, LANES, t), lambda i, h, *_: (i, h, 0))],
        out_specs=pl.BlockSpec((1, t, LANES), lambda i, h, *_: (i, 0, h)),
        scratch_shapes=[pltpu.VMEM((LANES, 2 * t), BF16), pltpu.VMEM((LANES + SUM_ROWS, t), BF16),
                        pltpu.VMEM((2, tq, 2 * tq), F32), pltpu.VMEM((2, 1, 2 * tq), F32),
                        pltpu.VMEM((1, 2 * tq), F32), pltpu.VMEM((n_q, LANES + SUM_ROWS, 2 * tq), F32)])
    return pl.pallas_call(
        kern, grid_spec=grid_spec, out_shape=jax.ShapeDtypeStruct((b, t, hw), BF16),
        compiler_params=_cparams(("parallel", "parallel")), name="diff_attn_prompt",
    )(qidx, kidx, vec(lq1), vec(lk1), vec(lq2), vec(lk2), subln.reshape(LANES, 1), bias, qt, k, vt)


DEC_ROWS = 2 * SUBLANES


def _attn_decode_kernel(pt_ref, lq1_ref, lk1_ref, lq2_ref, lk2_ref, subln_ref, q_ref, kn_ref, vn_ref, *rest,
                        pages, n_heads, t_new, lam_init):
    k_refs = rest[:pages]
    v_refs = rest[pages:2 * pages]
    o_ref = rest[2 * pages]
    m_ref, l_ref, acc_ref, vh_ref = rest[2 * pages + 1:]
    j = pl.program_id(1)

    @pl.when(j == 0)
    def _():
        m_ref[...] = jnp.full(m_ref.shape, NEG_BIG, F32)
        l_ref[...] = jnp.zeros(l_ref.shape, F32)
        acc_ref[...] = jnp.zeros(acc_ref.shape, F32)

    page = k_refs[0].shape[2]
    for p, vr in enumerate(v_refs):
        v_heads = jnp.swapaxes(vr[0].reshape(page, n_heads, LANES), 0, 1)
        vh_ref[:, p * page:(p + 1) * page, :] = v_heads.astype(BF16)
    head_lanes = lambda h: slice(h * LANES, (h + 1) * LANES)
    head_rows = lambda h: slice(h * DEC_ROWS, (h + 1) * DEC_ROWS)

    def softmax_step(s_heads, v_of_head):
        s = jnp.concatenate(s_heads, axis=0)
        m_old = m_ref[...]
        m_new = jnp.maximum(m_old, jnp.max(s, axis=-1, keepdims=True))
        alpha = jnp.exp2(m_old - m_new)
        p = jnp.exp2(s - m_new)
        l_ref[...] = alpha * l_ref[...] + jnp.sum(p, axis=-1, keepdims=True)
        pb = p.astype(BF16)
        pv = [jnp.dot(pb[head_rows(h)], v_of_head(h), preferred_element_type=F32) for h in range(n_heads)]
        acc_ref[...] = alpha * acc_ref[...] + jnp.concatenate(pv, axis=0)
        m_ref[...] = m_new

    s_heads = [jnp.dot(q_ref[0, h],
                       jnp.concatenate([kr[0, head_lanes(h), :].astype(BF16) for kr in k_refs], axis=1),
                       preferred_element_type=F32) for h in range(n_heads)]
    softmax_step(s_heads, lambda h: vh_ref[h])

    @pl.when(j == pl.num_programs(1) - 1)
    def _():
        lam = _diff_lambda(lq1_ref, lk1_ref, lq2_ref, lk2_ref, lam_init)
        n_new = kn_ref.shape[1]
        qt = lax.broadcasted_iota(jnp.int32, (DEC_ROWS, n_new), 0) % SUBLANES
        kt_pos = lax.broadcasted_iota(jnp.int32, (DEC_ROWS, n_new), 1)
        visible = (kt_pos <= qt) & (kt_pos < t_new)
        s_new = [jnp.where(visible,
                           lax.dot_general(q_ref[0, h], kn_ref[0, :, head_lanes(h)], (((1,), (1,)), ((), ())),
                                           preferred_element_type=F32), NEG_BIG) for h in range(n_heads)]
        softmax_step(s_new, lambda h: vn_ref[0, :, head_lanes(h)])
        n = acc_ref[...] / l_ref[...]
        for h in range(n_heads):
            nh_ = n[head_rows(h)]
            o_ref[0, :, head_lanes(h)] = _head_out(nh_[0:SUBLANES], nh_[SUBLANES:DEC_ROWS], lam, subln_ref[...],
                                                   1.0 - lam_init)


def _attn_decode(qd, k_new, v_new, cache_k, cache_v, page_table, lq1, lk1, lq2, lk2, subln, lam_init, *,
                 pages, t_new):
    b, nh = qd.shape[0], qd.shape[1]
    hw = nh * LANES
    n_pages = page_table.shape[1]
    assert n_pages % pages == 0
    vec = lambda a: a.reshape(1, -1)
    small = lambda w: pl.BlockSpec((1, w), lambda i, j, pt: (0, 0))

    def page_spec(cache, p):
        return pl.BlockSpec((1,) + cache.shape[1:], lambda i, j, pt: (pt[i, j * pages + p], 0, 0))

    kern = functools.partial(_attn_decode_kernel, pages=pages, n_heads=nh, t_new=t_new, lam_init=lam_init)
    grid_spec = pltpu.PrefetchScalarGridSpec(
        num_scalar_prefetch=1, grid=(b, n_pages // pages),
        in_specs=[small(lq1.size), small(lk1.size), small(lq2.size), small(lk2.size), small(LANES),
                  pl.BlockSpec((1, nh, DEC_ROWS, LANES), lambda i, j, pt: (i, 0, 0, 0)),
                  pl.BlockSpec((1, k_new.shape[1], hw), lambda i, j, pt: (i, 0, 0)),
                  pl.BlockSpec((1, v_new.shape[1], hw), lambda i, j, pt: (i, 0, 0))]
                 + [page_spec(cache_k, p) for p in range(pages)] + [page_spec(cache_v, p) for p in range(pages)],
        out_specs=pl.BlockSpec((1, SUBLANES, hw), lambda i, j, pt: (i, 0, 0)),
        scratch_shapes=[pltpu.VMEM((nh * DEC_ROWS, 1), F32), pltpu.VMEM((nh * DEC_ROWS, 1), F32),
                        pltpu.VMEM((nh * DEC_ROWS, LANES), F32),
                        pltpu.VMEM((nh, pages * cache_k.shape[2], LANES), BF16)])
    return pl.pallas_call(
        kern, grid_spec=grid_spec, out_shape=jax.ShapeDtypeStruct((b, SUBLANES, hw), F32),
        compiler_params=_cparams(("parallel", "arbitrary")), name="diff_attn_decode",
    )(page_table, vec(lq1), vec(lk1), vec(lq2), vec(lk2), vec(subln), qd, k_new, v_new,
      *([cache_k] * pages), *([cache_v] * pages))


def _rope_tables(pos, head_dim, batch):
    half = head_dim // 2
    inv = ROPE_THETA ** (-jnp.arange(half, dtype=F32) / half)
    ang = pos.astype(F32)[:, None] * inv[None, :]
    cos, sin = jnp.cos(ang), jnp.sin(ang)
    reps = LANES // head_dim
    cos_t = jnp.tile(jnp.concatenate([cos, cos], axis=1), (batch, reps))
    sin_t = jnp.tile(jnp.concatenate([-sin, sin], axis=1), (batch, reps))
    return cos_t, sin_t


def _plain(*dtypes):
    return [(dt, False) for dt in dtypes]


class _Group:
    def __init__(self, x, pos, conv0, h0, decode, cfg, head_dim):
        self.b, self.t, d = x.shape
        self.m = self.b * self.t
        self.x = x.reshape(self.m, d)
        self.cos, self.sin = _rope_tables(pos, head_dim, self.b)
        self.conv0, self.h0, self.decode, self.cfg = conv0, h0, decode, cfg
        self.prompt = decode is None
        self.new_bufs, self.new_hs = [], []
        self.kb = self.vb = self.k_out = self.v_new = None

    def mixer_recurrent(self, gate, u, l, p):
        b, t = self.b, self.t
        width = u.shape[1]
        t_pad = -(-t // SUBLANES) * SUBLANES
        u3, g3 = u.reshape(b, t, width), gate.reshape(b, t, width)
        if t_pad != t:
            u3 = jnp.pad(u3, ((0, 0), (0, t_pad - t), (0, 0)))
            g3 = jnp.pad(g3, ((0, 0), (0, t_pad - t), (0, 0)))
        hg, nb, nh = _conv_rglru(u3, g3, self.conv0[l], self.h0[l], p['rg_conv_w'][l], p['rg_conv_b'][l],
                                 p['rg_w_a'][l], p['rg_b_a'][l], p['rg_w_i'][l], p['rg_b_i'][l],
                                 p['rg_lambda'][l], tc=self.cfg['tc'], t_valid=t)
        self.new_bufs.append(nb)
        self.new_hs.append(nh.reshape(b, width))
        return hg[:, :t].reshape(self.m, width)

    def shared_kv(self, p, n_heads, head_dim):
        b, t, cfg = self.b, self.t, self.cfg
        qk_width = n_heads * 2 * head_dim
        assert p['w_kv'].shape[1] == 2 * qk_width
        kv_ws = dict(ws=[p['w_kv'], p['w_kv']], n=qk_width, col_starts=[0, qk_width])
        common = dict(extras=[self.cos, self.sin], tm=cfg['tm'], tn=cfg['tn'], name="shared_kv", **kv_ws)
        if self.prompt:
            k_t, self.kb, self.v_new, self.vb = _norm_matmul(
                self.x, p['kv_norm'], outs=[(F32, True), (BF16, False), (F32, False), (BF16, True)],
                epi=_epi_kv_transposed, seq_len=t, **common)
            self.k_out = k_t.reshape(b, n_heads, 2, head_dim, t).transpose(0, 4, 1, 2, 3)
        else:
            k_new, self.kb, self.v_new, self.vb = _norm_matmul(
                self.x, p['kv_norm'], outs=_plain(F32, BF16, F32, BF16), epi=_epi_kv, **common)
            self.k_out = k_new.reshape(b, t, n_heads, 2, head_dim)

    def mixer_attention(self, l, j, p, n_heads, head_dim):
        b, t, m, cfg = self.b, self.t, self.m, self.cfg
        lam_init = 0.8 - 0.6 * math.exp(-0.3 * l)
        (q,) = _norm_matmul(self.x, p['norm_mix'][l], [(p['dif_w_q'], j)], [self.cos, self.sin],
                            [(BF16, self.prompt)],
                            functools.partial(_epi_q, scale=head_dim ** -0.5 * LOG2_E, transposed=self.prompt),
                            tm=cfg['tm'], tn=cfg['tn'], name="q_proj", seq_len=t)
        lams = (p['dif_lq1'][j], p['dif_lk1'][j], p['dif_lq2'][j], p['dif_lk2'][j], p['dif_subln'][j])
        if self.prompt:
            o = _attn_prompt(q, self.kb.reshape(b, t, -1), self.vb, *lams, lam_init, tq=cfg['tq'])
            return o.reshape(m, -1)
        cache_k, cache_v, page_table = self.decode
        q4 = q.reshape(b, t, n_heads, LANES).transpose(0, 2, 1, 3)
        lane = jnp.arange(LANES)
        zpad = jnp.zeros((b, n_heads, SUBLANES - t, LANES), BF16)
        qd = jnp.concatenate([jnp.where(lane < head_dim, q4, 0).astype(BF16), zpad,
                              jnp.where(lane >= head_dim, q4, 0).astype(BF16), zpad], axis=2)
        pad_new = lambda a: jnp.pad(a.reshape(b, t, -1), ((0, 0), (0, DEC_ROWS - t), (0, 0)))
        o = _attn_decode(qd, pad_new(self.kb), pad_new(self.vb), cache_k, cache_v, page_table, *lams, lam_init,
                         pages=cfg['pages'], t_new=t)
        return o[:, :t].reshape(m, -1).astype(BF16)

    def outputs(self, p, n_heads):
        d = self.x.shape[1]
        y = _rmsnorm(self.x, p['norm_final'], tm=self.cfg['tm_norm']).reshape(self.b, self.t, d)
        v_out = self.v_new.reshape(self.b, self.t, n_heads, -1)
        return y, self.k_out, v_out, jnp.stack(self.new_bufs), jnp.stack(self.new_hs)


def _trunks(main, rider, p):
    depth = p['norm_mix'].shape[0]
    n_a = p['rg_w_x'].shape[0]
    head_dim = p['dif_lq1'].shape[1]
    n_heads = p['dif_w_q'].shape[2] // (2 * head_dim)
    cfg = main.cfg
    tm, tn = cfg['tm'], cfg['tn']
    for l in range(depth):
        if l < n_a:
            gate, u, gate_r, u_r = _norm_matmul(
                main.x, p['norm_mix'][l], [(p['rg_w_gate'], l), (p['rg_w_x'], l)], [], _plain(F32, F32),
                _epi_recurrent_in, tm=tm, tn=tn, name="recurrent_in", rider=rider.x)
            mixed, mixed_r = main.mixer_recurrent(gate, u, l, p), rider.mixer_recurrent(gate_r, u_r, l, p)
            w_out = (p['rg_w_out'], l)
        else:
            j = l - n_a
            if j == 0:
                main.shared_kv(p, n_heads, head_dim)
                rider.shared_kv(p, n_heads, head_dim)
            mixed = main.mixer_attention(l, j, p, n_heads, head_dim)
            mixed_r = rider.mixer_attention(l, j, p, n_heads, head_dim)
            w_out = (p['dif_w_o'], j)
        main.x, rider.x = _matmul_residual(mixed, w_out, main.x, mixed_r, rider.x, tm=cfg['tm_out'], tn=tn,
                                           name="mixer_out")
        hmid, hmid_r = _norm_matmul(main.x, p['norm_ffn'][l], [(p['ffn_w_gate'], l), (p['ffn_w_up'], l)], [],
                                    _plain(BF16), _epi_swiglu, tm=tm, tn=tn, name="ffn_in", rider=rider.x)
        main.x, rider.x = _matmul_residual(hmid, (p['ffn_w_down'], l), main.x, hmid_r, rider.x,
                                           tm=cfg['tm_down'], tn=tn, name="ffn_down")
    return main.outputs(p, n_heads), rider.outputs(p, n_heads)


def kernel(x_prompt, x_sample, cache_k, cache_v, page_table, state_conv, state_rglru, norm_mix, norm_ffn, norm_final, rg_w_x, rg_w_gate, rg_conv_w, rg_conv_b, rg_w_a, rg_b_a, rg_w_i, rg_b_i, rg_lambda, rg_w_out, kv_norm, w_kv, dif_w_q, dif_lq1, dif_lk1, dif_lq2, dif_lk2, dif_subln, dif_w_o, ffn_w_gate, ffn_w_up, ffn_w_down):
    p = dict(norm_mix=norm_mix, norm_ffn=norm_ffn, norm_final=norm_final, rg_w_x=rg_w_x,
             rg_w_gate=rg_w_gate, rg_conv_w=rg_conv_w, rg_conv_b=rg_conv_b, rg_w_a=rg_w_a,
             rg_b_a=rg_b_a, rg_w_i=rg_w_i, rg_b_i=rg_b_i, rg_lambda=rg_lambda, rg_w_out=rg_w_out,
             kv_norm=kv_norm, w_kv=w_kv, dif_w_q=dif_w_q, dif_lq1=dif_lq1, dif_lk1=dif_lk1,
             dif_lq2=dif_lq2, dif_lk2=dif_lk2, dif_subln=dif_subln, dif_w_o=dif_w_o,
             ffn_w_gate=ffn_w_gate, ffn_w_up=ffn_w_up, ffn_w_down=ffn_w_down)
    for name in ('ffn_w_down', 'rg_w_gate', 'rg_w_x', 'rg_w_out', 'w_kv', 'dif_w_q', 'dif_w_o'):
        p[name] = p[name].astype(BF16)

    n_a = rg_w_x.shape[0]
    width = rg_w_x.shape[2]
    head_dim = dif_lq1.shape[1]
    b_p, t_p = x_prompt.shape[0], x_prompt.shape[1]
    cfg_p = dict(tm=1024, tn=512, tm_out=2048, tm_down=1024, tm_norm=512, tc=256, tq=512)
    conv0 = jnp.zeros((n_a, b_p, CONV_WIDTH - 1, width), F32)
    h0 = jnp.zeros((n_a, b_p, width), F32)
    prompt = _Group(x_prompt, jnp.arange(t_p, dtype=jnp.int32), conv0, h0, None, cfg_p, head_dim)

    b_s, t_s = x_sample.shape[0], x_sample.shape[1]
    n_phys, page = cache_k.shape[0], cache_k.shape[1]
    past_len = page_table.shape[1] * page
    cfg_s = dict(tm=b_s * t_s, tn=512, tm_norm=b_s * t_s, tc=SUBLANES, pages=8)
    k_pages = jnp.transpose(cache_k, (0, 2, 3, 4, 1)).reshape(n_phys, -1, page)
    v_pages = cache_v.reshape(n_phys, page * cache_v.shape[2], cache_v.shape[3])
    pos_s = past_len + jnp.arange(t_s, dtype=jnp.int32)
    sample = _Group(x_sample, pos_s, state_conv, state_rglru, (k_pages, v_pages, page_table), cfg_s, head_dim)

    (y_p, k_p, v_p, conv_p, h_p), (y_s, k_s, v_s, conv_s, h_s) = _trunks(prompt, sample, p)
    return (y_p, y_s, k_p, v_p, conv_p, h_p, k_s, v_s, conv_s, h_s)
```

```python
import functools
import math

import jax
import jax.numpy as jnp
from jax import lax
from jax.experimental import pallas as pl
from jax.experimental.pallas import tpu as pltpu

F32 = jnp.float32
BF16 = jnp.bfloat16

EPS = 1e-6
RG_C = 8.0
ROPE_THETA = 10000.0
CONV_WIDTH = 4
LANES = 128
SUBLANES = 8
NEG_BIG = -1e30
VMEM_LIMIT = 56 * 1024 * 1024


def _cparams(sem):
    return pltpu.CompilerParams(dimension_semantics=sem, vmem_limit_bytes=VMEM_LIMIT)


def _weight_operand(w, rows, tn, first_block=0):
    if isinstance(w, tuple):
        stacked, layer = w
        return stacked, pl.BlockSpec((None, rows, tn), lambda i, j: (layer, 0, first_block + j))
    return w, pl.BlockSpec((rows, tn), lambda i, j: (0, first_block + j))


def _weight_cols(w):
    return w[0].shape[2] if isinstance(w, tuple) else w.shape[1]


def _rmsnorm_bf16(x, g):
    ms = jnp.mean(x * x, axis=-1, keepdims=True)
    return ((x * lax.rsqrt(ms + EPS)) * g).astype(BF16)


def _nmm_kernel(*refs, n_w, n_extra, n_out, epi, has_rider):
    x_ref, g_ref = refs[0], refs[1]
    w_refs = refs[2:2 + n_w]
    e_refs = refs[2 + n_w:2 + n_w + n_extra]
    pos = 2 + n_w + n_extra
    if has_rider:
        xr_ref = refs[pos]
        pos += 1
    o_refs = refs[pos:pos + n_out]
    i, j = pl.program_id(0), pl.program_id(1)

    @pl.when(j == 0)
    def _():
        refs[-1][...] = _rmsnorm_bf16(x_ref[...], g_ref[...])

    xn = refs[-1][...]
    w_tiles = [w[...].astype(BF16) for w in w_refs]
    epi([jnp.dot(xn, w, preferred_element_type=F32) for w in w_tiles], e_refs, o_refs)

    if has_rider:
        ro_refs = refs[pos + n_out:pos + 2 * n_out]
        xrn_ref = refs[-2]

        @pl.when(i == 0)
        def _():
            @pl.when(j == 0)
            def _():
                xrn_ref[...] = _rmsnorm_bf16(xr_ref[...], g_ref[...])

            xrn = xrn_ref[...]
            epi([jnp.dot(xrn, w, preferred_element_type=F32) for w in w_tiles], e_refs, [o.at[0] for o in ro_refs])

        @pl.when(i != 0)
        def _():
            for o in ro_refs:
                o[...] = jnp.zeros(o.shape, o.dtype)


def _norm_matmul(x, g, ws, extras, outs, epi, *, tm, tn, name, seq_len=None, n=None, col_starts=None,
                 rider=None):
    m, d = x.shape
    n = _weight_cols(ws[0]) if n is None else n
    col_starts = [0] * len(ws) if col_starts is None else col_starts
    tm = min(tm, m)
    grid = (m // tm, n // tn)
    in_specs = [pl.BlockSpec((tm, d), lambda i, j: (i, 0)),
                pl.BlockSpec((1, d), lambda i, j: (0, 0))]
    w_arrays = []
    for w, c in zip(ws, col_starts):
        arr, spec = _weight_operand(w, d, tn, c // tn)
        w_arrays.append(arr)
        in_specs.append(spec)
    in_specs += [pl.BlockSpec((tm, LANES), lambda i, j: (i, 0)) for _ in extras]
    out_specs, out_shape = [], []
    for dt, transposed in outs:
        if transposed:
            per_seq = seq_len // tm
            out_specs.append(pl.BlockSpec((1, tn, tm), lambda i, j: (i // per_seq, j, i % per_seq)))
            out_shape.append(jax.ShapeDtypeStruct((m // seq_len, n, seq_len), dt))
        else:
            out_specs.append(pl.BlockSpec((tm, tn), lambda i, j: (i, j)))
            out_shape.append(jax.ShapeDtypeStruct((m, n), dt))
    scratch = [pltpu.VMEM((tm, d), BF16)]
    operands = [x, g.reshape(1, d), *w_arrays, *extras]
    if rider is not None:
        assert not extras and not any(transposed for _, transposed in outs)
        mr = rider.shape[0]
        in_specs.append(pl.BlockSpec((mr, d), lambda i, j: (0, 0)))
        operands.append(rider)
        for dt, _ in outs:
            out_specs.append(pl.BlockSpec((1, mr, tn), lambda i, j: (i, 0, j)))
            out_shape.append(jax.ShapeDtypeStruct((grid[0], mr, n), dt))
        scratch.insert(0, pltpu.VMEM((mr, d), BF16))
    kern = functools.partial(_nmm_kernel, n_w=len(ws), n_extra=len(extras), n_out=len(outs), epi=epi,
                             has_rider=rider is not None)
    res = pl.pallas_call(
        kern, grid=grid, in_specs=in_specs, out_specs=out_specs, out_shape=out_shape,
        scratch_shapes=scratch, compiler_params=_cparams(("arbitrary", "arbitrary")), name=name,
    )(*operands)
    if rider is not None:
        res = list(res[:len(outs)]) + [r[0] for r in res[len(outs):]]
    return res


def _epi_recurrent_in(accs, e_refs, o_refs):
    o_refs[0][...] = jax.nn.gelu(accs[0])
    o_refs[1][...] = accs[1]


def _epi_swiglu(accs, e_refs, o_refs):
    o_refs[0][...] = (jax.nn.silu(accs[0]) * accs[1]).astype(BF16)


def _rope_slab(y, cos, sin_signed):
    lane = lax.broadcasted_iota(jnp.int32, y.shape, 1)
    first_half = (lane % 64) < 32
    rot = jnp.where(first_half, pltpu.roll(y, LANES - 32, 1), pltpu.roll(y, 32, 1))
    return y * cos + rot * sin_signed


def _epi_q(accs, e_refs, o_refs, *, scale, transposed):
    cos, sin_signed = e_refs[0][...], e_refs[1][...]
    for s in range(accs[0].shape[1] // LANES):
        sl = slice(s * LANES, (s + 1) * LANES)
        q = _rope_slab(accs[0][:, sl], cos, sin_signed) * scale
        if transposed:
            o_refs[0][0, sl, :] = q.T.astype(BF16)
        else:
            o_refs[0][:, sl] = q.astype(BF16)


def _epi_kv(accs, e_refs, o_refs):
    cos, sin_signed = e_refs[0][...], e_refs[1][...]
    for s in range(accs[0].shape[1] // LANES):
        sl = slice(s * LANES, (s + 1) * LANES)
        k = _rope_slab(accs[0][:, sl], cos, sin_signed)
        o_refs[0][:, sl] = k
        o_refs[1][:, sl] = k.astype(BF16)
    o_refs[2][...] = accs[1]
    o_refs[3][...] = accs[1].astype(BF16)


def _epi_kv_transposed(accs, e_refs, o_refs):
    cos, sin_signed = e_refs[0][...], e_refs[1][...]
    for s in range(accs[0].shape[1] // LANES):
        sl = slice(s * LANES, (s + 1) * LANES)
        k = _rope_slab(accs[0][:, sl], cos, sin_signed)
        o_refs[0][0, sl, :] = k.T
        o_refs[1][:, sl] = k.astype(BF16)
        o_refs[3][0, sl, :] = accs[1][:, sl].T.astype(BF16)
    o_refs[2][...] = accs[1]


def _mm_res_kernel(a_ref, w_ref, r_ref, ar_ref, rr_ref, o_ref, or_ref):
    w = w_ref[...].astype(BF16)
    o_ref[...] = r_ref[...] + jnp.dot(a_ref[...], w, preferred_element_type=F32)
    i = pl.program_id(0)

    @pl.when(i == 0)
    def _():
        or_ref[0] = rr_ref[...] + jnp.dot(ar_ref[...], w, preferred_element_type=F32)

    @pl.when(i != 0)
    def _():
        or_ref[...] = jnp.zeros(or_ref.shape, or_ref.dtype)


def _matmul_residual(a, w, res, a_rider, res_rider, *, tm, tn, name):
    m, k = a.shape
    mr = a_rider.shape[0]
    n = _weight_cols(w)
    w, w_spec = _weight_operand(w, k, tn)
    tm = min(tm, m)
    out, out_rider = pl.pallas_call(
        _mm_res_kernel, grid=(m // tm, n // tn),
        in_specs=[pl.BlockSpec((tm, k), lambda i, j: (i, 0)),
                  w_spec,
                  pl.BlockSpec((tm, tn), lambda i, j: (i, j)),
                  pl.BlockSpec((mr, k), lambda i, j: (0, 0)),
                  pl.BlockSpec((mr, tn), lambda i, j: (0, j))],
        out_specs=[pl.BlockSpec((tm, tn), lambda i, j: (i, j)),
                   pl.BlockSpec((1, mr, tn), lambda i, j: (i, 0, j))],
        out_shape=[jax.ShapeDtypeStruct((m, n), F32), jax.ShapeDtypeStruct((m // tm, mr, n), F32)],
        compiler_params=_cparams(("arbitrary", "arbitrary")), name=name,
    )(a, w, res, a_rider, res_rider)
    return out, out_rider[0]


def _rmsnorm_kernel(x_ref, g_ref, o_ref):
    x = x_ref[...]
    ms = jnp.mean(x * x, axis=-1, keepdims=True)
    o_ref[...] = (x * lax.rsqrt(ms + EPS)) * g_ref[...]


def _rmsnorm(x, g, *, tm):
    m, d = x.shape
    tm = min(tm, m)
    return pl.pallas_call(
        _rmsnorm_kernel, grid=(m // tm,),
        in_specs=[pl.BlockSpec((tm, d), lambda i: (i, 0)), pl.BlockSpec((1, d), lambda i: (0, 0))],
        out_specs=pl.BlockSpec((tm, d), lambda i: (i, 0)),
        out_shape=jax.ShapeDtypeStruct((m, d), F32),
        compiler_params=_cparams(("parallel",)), name="final_rmsnorm",
    )(x, g.reshape(1, d))


_EXPM1_SERIES_BOUND = 2.0 ** -6
_EXPM1_SERIES_TERMS = 4


def _neg_expm1_twice(half_x, exp_half_x):
    x = 2.0 * half_x
    poly = jnp.full_like(x, 1.0 / math.factorial(_EXPM1_SERIES_TERMS))
    for k in range(_EXPM1_SERIES_TERMS - 1, 0, -1):
        poly = poly * x + 1.0 / math.factorial(k)
    return jnp.where(x > -_EXPM1_SERIES_BOUND, -(poly * x), 1.0 - exp_half_x * exp_half_x)


def _scan_kernel(u_ref, gate_ref, cw_ref, cb_ref, wa_ref, ba_ref, wi_ref, bi_ref, lam_ref, conv0_ref, h0_ref,
                 hg_ref, convout_ref, hlast_ref, ubuf, a_s, b_s, hs, hcar, *, tc, last_row, n_gate_blocks):
    c = pl.program_id(1)
    width = u_ref.shape[2]
    pad = SUBLANES
    hist = CONV_WIDTH - 1

    @pl.when(c == 0)
    def _():
        ubuf[0:pad, :] = jnp.zeros((pad, width), F32)
        ubuf[pad - hist:pad, :] = conv0_ref[0]
        hcar[...] = jnp.broadcast_to(h0_ref[0], (SUBLANES, width))

    u = u_ref[0]
    ubuf[pad:pad + tc, :] = u
    cw = cw_ref[...]
    xc = cb_ref[...] + u * cw[0:1, :]
    for j in range(1, CONV_WIDTH):
        xc = xc + ubuf[pad - j:pad - j + tc, :] * cw[j:j + 1, :]

    neg_c_softplus = -RG_C * jax.nn.softplus(-lam_ref[...])
    gb = width // n_gate_blocks
    for n in range(n_gate_blocks):
        sl = slice(n * gb, (n + 1) * gb)
        xs = xc[:, sl]
        xb = xs.astype(BF16)
        r = jax.nn.sigmoid(jnp.dot(xb, wa_ref[n].astype(BF16), preferred_element_type=F32) + ba_ref[:, sl])
        i = jax.nn.sigmoid(jnp.dot(xb, wi_ref[n].astype(BF16), preferred_element_type=F32) + bi_ref[:, sl])
        log_a = r * neg_c_softplus[:, sl]
        a = jnp.exp(log_a)
        a_s[:, sl] = a
        b_s[:, sl] = jnp.sqrt(_neg_expm1_twice(log_a, a)) * i * xs

    row = lax.broadcasted_iota(jnp.int32, (SUBLANES, width), 0)

    def group(g, h):
        base = pl.multiple_of(g * SUBLANES, SUBLANES)
        a8 = a_s[pl.ds(base, SUBLANES), :]
        b8 = b_s[pl.ds(base, SUBLANES), :]
        out = jnp.zeros((SUBLANES, width), F32)
        for r in range(SUBLANES):
            hn = a8 * h + b8
            out = jnp.where(row == r, hn, out)
            h = jnp.broadcast_to(hn[r:r + 1, :], (SUBLANES, width))
        hs[pl.ds(base, SUBLANES), :] = out
        return h

    hcar[...] = lax.fori_loop(0, tc // SUBLANES, group, hcar[...])
    hg_ref[0] = (hs[...] * gate_ref[0]).astype(BF16)
    ubuf[0:pad, :] = ubuf[tc:tc + pad, :]

    @pl.when(c == pl.num_programs(1) - 1)
    def _():
        convout_ref[0] = ubuf[pad + last_row + 1 - hist:pad + last_row + 1, :]
        hlast_ref[0] = hs[last_row:last_row + 1, :]


def _conv_rglru(u, gate, conv0, h0, cw, cb, wa, ba, wi, bi, lam, *, tc, t_valid):
    b, t, width = u.shape
    tc = min(tc, t)
    n_chunks = t // tc
    last_row = (t_valid - 1) - (n_chunks - 1) * tc
    nb = wa.shape[0]
    row2 = lambda v: v.reshape(1, width)
    kern = functools.partial(_scan_kernel, tc=tc, last_row=last_row, n_gate_blocks=nb)
    full2 = lambda shape: pl.BlockSpec(shape, lambda i, c: (0, 0))
    return pl.pallas_call(
        kern, grid=(b, n_chunks),
        in_specs=[pl.BlockSpec((1, tc, width), lambda i, c: (i, c, 0)),
                  pl.BlockSpec((1, tc, width), lambda i, c: (i, c, 0)),
                  full2((CONV_WIDTH, width)), full2((1, width)),
                  pl.BlockSpec(wa.shape, lambda i, c: (0, 0, 0)), full2((1, width)),
                  pl.BlockSpec(wi.shape, lambda i, c: (0, 0, 0)), full2((1, width)),
                  full2((1, width)),
                  pl.BlockSpec((1, CONV_WIDTH - 1, width), lambda i, c: (i, 0, 0)),
                  pl.BlockSpec((1, 1, width), lambda i, c: (i, 0, 0))],
        out_specs=[pl.BlockSpec((1, tc, width), lambda i, c: (i, c, 0)),
                   pl.BlockSpec((1, CONV_WIDTH - 1, width), lambda i, c: (i, 0, 0)),
                   pl.BlockSpec((1, 1, width), lambda i, c: (i, 0, 0))],
        out_shape=[jax.ShapeDtypeStruct((b, t, width), BF16),
                   jax.ShapeDtypeStruct((b, CONV_WIDTH - 1, width), F32),
                   jax.ShapeDtypeStruct((b, 1, width), F32)],
        scratch_shapes=[pltpu.VMEM((tc + SUBLANES, width), F32),
                        pltpu.VMEM((tc, width), F32), pltpu.VMEM((tc, width), F32),
                        pltpu.VMEM((tc, width), F32), pltpu.VMEM((SUBLANES, width), F32)],
        compiler_params=_cparams(("parallel", "arbitrary")), name="conv_rglru",
    )(u, gate, cw, row2(cb), wa, row2(ba), wi, row2(bi), row2(lam), conv0, h0.reshape(b, 1, width))


def _diff_lambda(lq1_ref, lk1_ref, lq2_ref, lk2_ref, lam_init):
    e1 = jnp.exp(jnp.sum(lq1_ref[...] * lk1_ref[...], axis=-1, keepdims=True))
    e2 = jnp.exp(jnp.sum(lq2_ref[...] * lk2_ref[...], axis=-1, keepdims=True))
    return e1 - e2 + lam_init


LOG2_E = math.log2(math.e)


def _online_softmax_step(s, v, m_ref, l_ref, acc_ref):
    m_old = m_ref[...]
    m_new = jnp.maximum(m_old, jnp.max(s, axis=-1, keepdims=True))
    alpha = jnp.exp2(m_old - m_new)
    p = jnp.exp2(s - m_new)
    l_ref[...] = alpha * l_ref[...] + jnp.sum(p, axis=-1, keepdims=True)
    acc_ref[...] = alpha * acc_ref[...] + jnp.dot(p.astype(BF16), v, preferred_element_type=F32)
    m_ref[...] = m_new


def _head_out(n0, n1, lam, subln, out_scale):
    o = n0 - lam * n1
    ms = jnp.mean(o * o, axis=-1, keepdims=True)
    return ((o * lax.rsqrt(ms + EPS)) * subln) * out_scale


SUM_ROWS = 16


def _attn_prompt_kernel(qidx_ref, kidx_ref, lq1_ref, lk1_ref, lq2_ref, lk2_ref, subln_ref, bias_ref,
                        qt_ref, k_ref, vt_ref, o_ref, qs_ref, vta_ref, s_ref, cmax_ref, m_ref, acc_ref, *,
                        tq, n_items, lam_init):
    n_q = qt_ref.shape[2] // tq
    feat = lax.broadcasted_iota(jnp.int32, (LANES, tq), 0)
    zero = jnp.zeros((LANES, tq), BF16)
    half = LANES // 2
    for qi in range(n_q):
        qt = qt_ref[0, :, qi * tq:(qi + 1) * tq]
        qs_ref[:, (2 * qi) * tq:(2 * qi + 1) * tq] = jnp.where(feat < half, qt, zero)
        qs_ref[:, (2 * qi + 1) * tq:(2 * qi + 2) * tq] = jnp.where(feat >= half, qt, zero)
    m_ref[...] = jnp.full(m_ref.shape, NEG_BIG, F32)
    acc_ref[...] = jnp.zeros(acc_ref.shape, F32)
    vta_ref[0:LANES, :] = vt_ref[0]
    vta_ref[LANES:, :] = jnp.ones((vta_ref.shape[0] - LANES, vta_ref.shape[1]), BF16)
    lam = _diff_lambda(lq1_ref, lk1_ref, lq2_ref, lk2_ref, lam_init)

    def produce(w, slot):
        qi, ki = qidx_ref[w], kidx_ref[w]
        kstart = pl.multiple_of(ki * tq, tq)
        qstart = pl.multiple_of(qi * (2 * tq), 2 * tq)
        s = jnp.dot(k_ref[0, pl.ds(kstart, tq), :], qs_ref[:, pl.ds(qstart, 2 * tq)],
                    preferred_element_type=F32)
        s = s + bias_ref[(qi == ki).astype(jnp.int32)]
        s_ref[slot] = s
        cmax_ref[slot] = jnp.max(s, axis=0, keepdims=True)

    def consume(w, slot):
        qi, ki = qidx_ref[w], kidx_ref[w]
        kstart = pl.multiple_of(ki * tq, tq)
        m_old = jnp.where(ki == 0, NEG_BIG, m_ref[...])
        m_new = jnp.maximum(m_old, cmax_ref[slot])
        alpha = jnp.exp2(m_old - m_new)
        p = jnp.exp2(s_ref[slot] - m_new)
        pv = jnp.dot(vta_ref[:, pl.ds(kstart, tq)], p.astype(BF16), preferred_element_type=F32)
        acc_ref[qi] = alpha * acc_ref[qi] + pv
        m_ref[...] = m_new

    produce(0, 0)
    per_trip = 4
    n_looped = max(n_items // per_trip - 1, 0)

    def trip(i, carry):
        for g in range(per_trip):
            produce(per_trip * i + g + 1, (g + 1) % 2)
            consume(per_trip * i + g, g % 2)
        return carry

    lax.fori_loop(0, n_looped, trip, 0)
    for w in range(per_trip * n_looped, n_items):
        if w + 1 < n_items:
            produce(w + 1, (w + 1) % 2)
        consume(w, w % 2)

    for qi in range(n_q):
        acc = acc_ref[qi]
        n = acc[0:LANES] * (1.0 / acc[LANES:LANES + 1])
        o = n[:, 0:tq] - lam * n[:, tq:2 * tq]
        ms = jnp.mean(o * o, axis=0, keepdims=True)
        y = ((o * lax.rsqrt(ms + EPS)) * subln_ref[...]) * (1.0 - lam_init)
        o_ref[0, qi * tq:(qi + 1) * tq, :] = y.T.astype(BF16)


def _attn_prompt(qt, k, vt, lq1, lk1, lq2, lk2, subln, lam_init, *, tq):
    b, hw, t = qt.shape
    nh = hw // LANES
    assert t % tq == 0
    n_q = t // tq
    items = [(qi, ki) for qi in range(n_q) for ki in range(qi + 1)]
    qidx = jnp.array([qi for qi, _ in items], jnp.int32)
    kidx = jnp.array([ki for _, ki in items], jnp.int32)
    kpos = lax.broadcasted_iota(jnp.int32, (tq, 2 * tq), 0)
    qpos = lax.broadcasted_iota(jnp.int32, (tq, 2 * tq), 1) % tq
    bias = jnp.stack([jnp.zeros((tq, 2 * tq), F32), jnp.where(kpos <= qpos, 0.0, NEG_BIG).astype(F32)])
    vec = lambda a: a.reshape(1, -1)
    small = lambda w: pl.BlockSpec((1, w), lambda i, h, *_: (0, 0))
    kern = functools.partial(_attn_prompt_kernel, tq=tq, n_items=len(items), lam_init=lam_init)
    grid_spec = pltpu.PrefetchScalarGridSpec(
        num_scalar_prefetch=2, grid=(b, nh),
        in_specs=[small(lq1.size), small(lk1.size), small(lq2.size), small(lk2.size),
                  pl.BlockSpec((LANES, 1), lambda i, h, *_: (0, 0)),
                  pl.BlockSpec((2, tq, 2 * tq), lambda i, h, *_: (0, 0, 0)),
                  pl.BlockSpec((1, LANES, t), lambda i, h, *_: (i, h, 0)),
                  pl.BlockSpec((1, t, LANES), lambda i, h, *_: (i, 0, h)),
                  pl.BlockSpec((1, LANES, t), lambda i, h, *_: (i, h, 0))],
        out_specs=pl.BlockSpec((1, t, LANES), lambda i, h, *_: (i, 0, h)),
        scratch_shapes=[pltpu.VMEM((LANES, 2 * t), BF16), pltpu.VMEM((LANES + SUM_ROWS, t), BF16),
                        pltpu.VMEM((2, tq, 2 * tq), F32), pltpu.VMEM((2, 1, 2 * tq), F32),
                        pltpu.VMEM((1, 2 * tq), F32), pltpu.VMEM((n_q, LANES + SUM_ROWS, 2 * tq), F32)])
    return pl.pallas_call(
        kern, grid_spec=grid_spec, out_shape=jax.ShapeDtypeStruct((b, t, hw), BF16),
        compiler_params=_cparams(("parallel", "parallel")), name="diff_attn_prompt",
    )(qidx, kidx, vec(lq1), vec(lk1), vec(lq2), vec(lk2), subln.reshape(LANES, 1), bias, qt, k, vt)


DEC_ROWS = 2 * SUBLANES


PAGE_RING = 3


def _page_copies(pt_ref, kc_hbm, vc_hbm, kbuf, vbuf, sem, step, pages):
    n_j = pl.num_programs(1)
    b, j = step // n_j, step % n_j
    slot = step % PAGE_RING
    copies = []
    for p in range(pages):
        pid = pt_ref[b, j * pages + p]
        copies.append(pltpu.make_async_copy(kc_hbm.at[pid], kbuf.at[slot, p], sem.at[slot]))
        copies.append(pltpu.make_async_copy(vc_hbm.at[pid], vbuf.at[slot, p], sem.at[slot]))
    return copies


def _attn_decode_kernel(pt_ref, lq1_ref, lk1_ref, lq2_ref, lk2_ref, subln_ref, q_ref, kn_ref, vn_ref,
                        kc_hbm, vc_hbm, o_ref, m_ref, l_ref, acc_ref, vh_ref, kbuf, vbuf, sem, *,
                        pages, n_heads, t_new, lam_init):
    j = pl.program_id(1)
    n_steps = pl.num_programs(0) * pl.num_programs(1)
    step = pl.program_id(0) * pl.num_programs(1) + j
    copies_of = functools.partial(_page_copies, pt_ref, kc_hbm, vc_hbm, kbuf, vbuf, sem, pages=pages)

    @pl.when(step == 0)
    def _():
        for ahead in range(PAGE_RING - 1):
            @pl.when(ahead < n_steps)
            def _():
                for c in copies_of(jnp.int32(ahead)):
                    c.start()

    @pl.when(step + (PAGE_RING - 1) < n_steps)
    def _():
        for c in copies_of(step + (PAGE_RING - 1)):
            c.start()

    for c in copies_of(step):
        c.wait()
    slot = step % PAGE_RING

    @pl.when(j == 0)
    def _():
        m_ref[...] = jnp.full(m_ref.shape, NEG_BIG, F32)
        l_ref[...] = jnp.zeros(l_ref.shape, F32)
        acc_ref[...] = jnp.zeros(acc_ref.shape, F32)

    page = kbuf.shape[3]
    for p in range(pages):
        v_heads = jnp.swapaxes(vbuf[slot, p].reshape(page, n_heads, LANES), 0, 1)
        vh_ref[:, p * page:(p + 1) * page, :] = v_heads.astype(BF16)
    head_lanes = lambda h: slice(h * LANES, (h + 1) * LANES)
    head_rows = lambda h: slice(h * DEC_ROWS, (h + 1) * DEC_ROWS)

    def softmax_step(s_heads, v_of_head):
        s = jnp.concatenate(s_heads, axis=0)
        m_old = m_ref[...]
        m_new = jnp.maximum(m_old, jnp.max(s, axis=-1, keepdims=True))
        alpha = jnp.exp2(m_old - m_new)
        p = jnp.exp2(s - m_new)
        l_ref[...] = alpha * l_ref[...] + jnp.sum(p, axis=-1, keepdims=True)
        pb = p.astype(BF16)
        pv = [jnp.dot(pb[head_rows(h)], v_of_head(h), preferred_element_type=F32) for h in range(n_heads)]
        acc_ref[...] = alpha * acc_ref[...] + jnp.concatenate(pv, axis=0)
        m_ref[...] = m_new

    s_heads = [jnp.dot(q_ref[0, h],
                       jnp.concatenate([kbuf[slot, p, head_lanes(h), :].astype(BF16) for p in range(pages)], axis=1),
                       preferred_element_type=F32) for h in range(n_heads)]
    softmax_step(s_heads, lambda h: vh_ref[h])

    @pl.when(j == pl.num_programs(1) - 1)
    def _():
        lam = _diff_lambda(lq1_ref, lk1_ref, lq2_ref, lk2_ref, lam_init)
        n_new = kn_ref.shape[1]
        qt = lax.broadcasted_iota(jnp.int32, (DEC_ROWS, n_new), 0) % SUBLANES
        kt_pos = lax.broadcasted_iota(jnp.int32, (DEC_ROWS, n_new), 1)
        visible = (kt_pos <= qt) & (kt_pos < t_new)
        s_new = [jnp.where(visible,
                           lax.dot_general(q_ref[0, h], kn_ref[0, :, head_lanes(h)], (((1,), (1,)), ((), ())),
                                           preferred_element_type=F32), NEG_BIG) for h in range(n_heads)]
        softmax_step(s_new, lambda h: vn_ref[0, :, head_lanes(h)])
        n = acc_ref[...] / l_ref[...]
        for h in range(n_heads):
            nh_ = n[head_rows(h)]
            o_ref[0, :, head_lanes(h)] = _head_out(nh_[0:SUBLANES], nh_[SUBLANES:DEC_ROWS], lam, subln_ref[...],
                                                   1.0 - lam_init)


def _attn_decode(qd, k_new, v_new, cache_k, cache_v, page_table, lq1, lk1, lq2, lk2, subln, lam_init, *,
                 pages, t_new):
    b, nh = qd.shape[0], qd.shape[1]
    hw = nh * LANES
    n_pages = page_table.shape[1]
    assert n_pages % pages == 0
    vec = lambda a: a.reshape(1, -1)
    small = lambda w: pl.BlockSpec((1, w), lambda i, j, pt: (0, 0))

    kern = functools.partial(_attn_decode_kernel, pages=pages, n_heads=nh, t_new=t_new, lam_init=lam_init)
    grid_spec = pltpu.PrefetchScalarGridSpec(
        num_scalar_prefetch=1, grid=(b, n_pages // pages),
        in_specs=[small(lq1.size), small(lk1.size), small(lq2.size), small(lk2.size), small(LANES),
                  pl.BlockSpec((1, nh, DEC_ROWS, LANES), lambda i, j, pt: (i, 0, 0, 0)),
                  pl.BlockSpec((1, k_new.shape[1], hw), lambda i, j, pt: (i, 0, 0)),
                  pl.BlockSpec((1, v_new.shape[1], hw), lambda i, j, pt: (i, 0, 0)),
                  pl.BlockSpec(memory_space=pl.ANY), pl.BlockSpec(memory_space=pl.ANY)],
        out_specs=pl.BlockSpec((1, SUBLANES, hw), lambda i, j, pt: (i, 0, 0)),
        scratch_shapes=[pltpu.VMEM((nh * DEC_ROWS, 1), F32), pltpu.VMEM((nh * DEC_ROWS, 1), F32),
                        pltpu.VMEM((nh * DEC_ROWS, LANES), F32),
                        pltpu.VMEM((nh, pages * cache_k.shape[2], LANES), BF16),
                        pltpu.VMEM((PAGE_RING, pages) + cache_k.shape[1:], F32),
                        pltpu.VMEM((PAGE_RING, pages) + cache_v.shape[1:], F32),
                        pltpu.SemaphoreType.DMA((PAGE_RING,))])
    return pl.pallas_call(
        kern, grid_spec=grid_spec, out_shape=jax.ShapeDtypeStruct((b, SUBLANES, hw), F32),
        compiler_params=_cparams(("arbitrary", "arbitrary")), name="diff_attn_decode",
    )(page_table, vec(lq1), vec(lk1), vec(lq2), vec(lk2), vec(subln), qd, k_new, v_new, cache_k, cache_v)


def _rope_tables(pos, head_dim, batch):
    half = head_dim // 2
    inv = ROPE_THETA ** (-jnp.arange(half, dtype=F32) / half)
    ang = pos.astype(F32)[:, None] * inv[None, :]
    cos, sin = jnp.cos(ang), jnp.sin(ang)
    reps = LANES // head_dim
    cos_t = jnp.tile(jnp.concatenate([cos, cos], axis=1), (batch, reps))
    sin_t = jnp.tile(jnp.concatenate([-sin, sin], axis=1), (batch, reps))
    return cos_t, sin_t


def _plain(*dtypes):
    return [(dt, False) for dt in dtypes]


class _Group:
    def __init__(self, x, pos, conv0, h0, decode, cfg, head_dim):
        self.b, self.t, d = x.shape
        self.m = self.b * self.t
        self.x = x.reshape(self.m, d)
        self.cos, self.sin = _rope_tables(pos, head_dim, self.b)
        self.conv0, self.h0, self.decode, self.cfg = conv0, h0, decode, cfg
        self.prompt = decode is None
        self.new_bufs, self.new_hs = [], []
        self.kb = self.vb = self.k_out = self.v_new = None

    def mixer_recurrent(self, gate, u, l, p):
        b, t = self.b, self.t
        width = u.shape[1]
        t_pad = -(-t // SUBLANES) * SUBLANES
        u3, g3 = u.reshape(b, t, width), gate.reshape(b, t, width)
        if t_pad != t:
            u3 = jnp.pad(u3, ((0, 0), (0, t_pad - t), (0, 0)))
            g3 = jnp.pad(g3, ((0, 0), (0, t_pad - t), (0, 0)))
        hg, nb, nh = _conv_rglru(u3, g3, self.conv0[l], self.h0[l], p['rg_conv_w'][l], p['rg_conv_b'][l],
                                 p['rg_w_a'][l], p['rg_b_a'][l], p['rg_w_i'][l], p['rg_b_i'][l],
                                 p['rg_lambda'][l], tc=self.cfg['tc'], t_valid=t)
        self.new_bufs.append(nb)
        self.new_hs.append(nh.reshape(b, width))
        return hg[:, :t].reshape(self.m, width)

    def shared_kv(self, p, n_heads, head_dim):
        b, t, cfg = self.b, self.t, self.cfg
        qk_width = n_heads * 2 * head_dim
        assert p['w_kv'].shape[1] == 2 * qk_width
        kv_ws = dict(ws=[p['w_kv'], p['w_kv']], n=qk_width, col_starts=[0, qk_width])
        common = dict(extras=[self.cos, self.sin], tm=cfg['tm'], tn=cfg['tn'], name="shared_kv", **kv_ws)
        if self.prompt:
            k_t, self.kb, self.v_new, self.vb = _norm_matmul(
                self.x, p['kv_norm'], outs=[(F32, True), (BF16, False), (F32, False), (BF16, True)],
                epi=_epi_kv_transposed, seq_len=t, **common)
            self.k_out = k_t.reshape(b, n_heads, 2, head_dim, t).transpose(0, 4, 1, 2, 3)
        else:
            k_new, self.kb, self.v_new, self.vb = _norm_matmul(
                self.x, p['kv_norm'], outs=_plain(F32, BF16, F32, BF16), epi=_epi_kv, **common)
            self.k_out = k_new.reshape(b, t, n_heads, 2, head_dim)

    def mixer_attention(self, l, j, p, n_heads, head_dim):
        b, t, m, cfg = self.b, self.t, self.m, self.cfg
        lam_init = 0.8 - 0.6 * math.exp(-0.3 * l)
        (q,) = _norm_matmul(self.x, p['norm_mix'][l], [(p['dif_w_q'], j)], [self.cos, self.sin],
                            [(BF16, self.prompt)],
                            functools.partial(_epi_q, scale=head_dim ** -0.5 * LOG2_E, transposed=self.prompt),
                            tm=cfg['tm'], tn=cfg['tn'], name="q_proj", seq_len=t)
        lams = (p['dif_lq1'][j], p['dif_lk1'][j], p['dif_lq2'][j], p['dif_lk2'][j], p['dif_subln'][j])
        if self.prompt:
            o = _attn_prompt(q, self.kb.reshape(b, t, -1), self.vb, *lams, lam_init, tq=cfg['tq'])
            return o.reshape(m, -1)
        cache_k, cache_v, page_table = self.decode
        q4 = q.reshape(b, t, n_heads, LANES).transpose(0, 2, 1, 3)
        lane = jnp.arange(LANES)
        zpad = jnp.zeros((b, n_heads, SUBLANES - t, LANES), BF16)
        qd = jnp.concatenate([jnp.where(lane < head_dim, q4, 0).astype(BF16), zpad,
                              jnp.where(lane >= head_dim, q4, 0).astype(BF16), zpad], axis=2)
        pad_new = lambda a: jnp.pad(a.reshape(b, t, -1), ((0, 0), (0, DEC_ROWS - t), (0, 0)))
        o = _attn_decode(qd, pad_new(self.kb), pad_new(self.vb), cache_k, cache_v, page_table, *lams, lam_init,
                         pages=cfg['pages'], t_new=t)
        return o[:, :t].reshape(m, -1).astype(BF16)

    def outputs(self, p, n_heads):
        d = self.x.shape[1]
        y = _rmsnorm(self.x, p['norm_final'], tm=self.cfg['tm_norm']).reshape(self.b, self.t, d)
        v_out = self.v_new.reshape(self.b, self.t, n_heads, -1)
        return y, self.k_out, v_out, jnp.stack(self.new_bufs), jnp.stack(self.new_hs)


def _trunks(main, rider, p):
    depth = p['norm_mix'].shape[0]
    n_a = p['rg_w_x'].shape[0]
    head_dim = p['dif_lq1'].shape[1]
    n_heads = p['dif_w_q'].shape[2] // (2 * head_dim)
    cfg = main.cfg
    tm, tn = cfg['tm'], cfg['tn']
    for l in range(depth):
        if l < n_a:
            gate, u, gate_r, u_r = _norm_matmul(
                main.x, p['norm_mix'][l], [(p['rg_w_gate'], l), (p['rg_w_x'], l)], [], _plain(F32, F32),
                _epi_recurrent_in, tm=tm, tn=tn, name="recurrent_in", rider=rider.x)
            mixed, mixed_r = main.mixer_recurrent(gate, u, l, p), rider.mixer_recurrent(gate_r, u_r, l, p)
            w_out = (p['rg_w_out'], l)
        else:
            j = l - n_a
            if j == 0:
                main.shared_kv(p, n_heads, head_dim)
                rider.shared_kv(p, n_heads, head_dim)
            mixed = main.mixer_attention(l, j, p, n_heads, head_dim)
            mixed_r = rider.mixer_attention(l, j, p, n_heads, head_dim)
            w_out = (p['dif_w_o'], j)
        main.x, rider.x = _matmul_residual(mixed, w_out, main.x, mixed_r, rider.x, tm=cfg['tm_out'], tn=tn,
                                           name="mixer_out")
        hmid, hmid_r = _norm_matmul(main.x, p['norm_ffn'][l], [(p['ffn_w_gate'], l), (p['ffn_w_up'], l)], [],
                                    _plain(BF16), _epi_swiglu, tm=tm, tn=tn, name="ffn_in", rider=rider.x)
        main.x, rider.x = _matmul_residual(hmid, (p['ffn_w_down'], l), main.x, hmid_r, rider.x,
                                           tm=cfg['tm_down'], tn=tn, name="ffn_down")
    return main.outputs(p, n_heads), rider.outputs(p, n_heads)


def kernel(x_prompt, x_sample, cache_k, cache_v, page_table, state_conv, state_rglru, norm_mix, norm_ffn, norm_final, rg_w_x, rg_w_gate, rg_conv_w, rg_conv_b, rg_w_a, rg_b_a, rg_w_i, rg_b_i, rg_lambda, rg_w_out, kv_norm, w_kv, dif_w_q, dif_lq1, dif_lk1, dif_lq2, dif_lk2, dif_subln, dif_w_o, ffn_w_gate, ffn_w_up, ffn_w_down):
    p = dict(norm_mix=norm_mix, norm_ffn=norm_ffn, norm_final=norm_final, rg_w_x=rg_w_x,
             rg_w_gate=rg_w_gate, rg_conv_w=rg_conv_w, rg_conv_b=rg_conv_b, rg_w_a=rg_w_a,
             rg_b_a=rg_b_a, rg_w_i=rg_w_i, rg_b_i=rg_b_i, rg_lambda=rg_lambda, rg_w_out=rg_w_out,
             kv_norm=kv_norm, w_kv=w_kv, dif_w_q=dif_w_q, dif_lq1=dif_lq1, dif_lk1=dif_lk1,
             dif_lq2=dif_lq2, dif_lk2=dif_lk2, dif_subln=dif_subln, dif_w_o=dif_w_o,
             ffn_w_gate=ffn_w_gate, ffn_w_up=ffn_w_up, ffn_w_down=ffn_w_down)
    p['ffn_w_down'] = ffn_w_down.astype(BF16)

    n_a = rg_w_x.shape[0]
    width = rg_w_x.shape[2]
    head_dim = dif_lq1.shape[1]
    b_p, t_p = x_prompt.shape[0], x_prompt.shape[1]
    cfg_p = dict(tm=1024, tn=512, tm_out=2048, tm_down=1024, tm_norm=512, tc=256, tq=512)
    conv0 = jnp.zeros((n_a, b_p, CONV_WIDTH - 1, width), F32)
    h0 = jnp.zeros((n_a, b_p, width), F32)
    prompt = _Group(x_prompt, jnp.arange(t_p, dtype=jnp.int32), conv0, h0, None, cfg_p, head_dim)

    b_s, t_s = x_sample.shape[0], x_sample.shape[1]
    n_phys, page = cache_k.shape[0], cache_k.shape[1]
    past_len = page_table.shape[1] * page
    cfg_s = dict(tm=b_s * t_s, tn=512, tm_norm=b_s * t_s, tc=SUBLANES, pages=4)
    k_pages = jnp.transpose(cache_k, (0, 2, 3, 4, 1)).reshape(n_phys, -1, page)
    v_pages = cache_v.reshape(n_phys, page * cache_v.shape[2], cache_v.shape[3])
    pos_s = past_len + jnp.arange(t_s, dtype=jnp.int32)
    sample = _Group(x_sample, pos_s, state_conv, state_rglru, (k_pages, v_pages, page_table), cfg_s, head_dim)

    (y_p, k_p, v_p, conv_p, h_p), (y_s, k_s, v_s, conv_s, h_s) = _trunks(prompt, sample, p)
    return (y_p, y_s, k_p, v_p, conv_p, h_p, k_s, v_s, conv_s, h_s)
```

```python
import functools
import math

import jax
import jax.numpy as jnp
from jax import lax
from jax.experimental import pallas as pl
from jax.experimental.pallas import tpu as pltpu

F32 = jnp.float32
BF16 = jnp.bfloat16

EPS = 1e-6
RG_C = 8.0
ROPE_THETA = 10000.0
CONV_WIDTH = 4
LANES = 128
SUBLANES = 8
NEG_BIG = -1e30
VMEM_LIMIT = 56 * 1024 * 1024


def _cparams(sem):
    return pltpu.CompilerParams(dimension_semantics=sem, vmem_limit_bytes=VMEM_LIMIT)


def _weight_operand(w, rows, tn, first_block=0):
    if isinstance(w, tuple):
        stacked, layer = w
        return stacked, pl.BlockSpec((None, rows, tn), lambda i, j: (layer, 0, first_block + j))
    return w, pl.BlockSpec((rows, tn), lambda i, j: (0, first_block + j))


def _weight_cols(w):
    return w[0].shape[2] if isinstance(w, tuple) else w.shape[1]


def _rmsnorm_bf16(x, g):
    ms = jnp.mean(x * x, axis=-1, keepdims=True)
    return ((x * lax.rsqrt(ms + EPS)) * g).astype(BF16)


def _nmm_kernel(*refs, n_w, n_extra, n_out, epi, has_rider):
    x_ref, g_ref = refs[0], refs[1]
    w_refs = refs[2:2 + n_w]
    e_refs = refs[2 + n_w:2 + n_w + n_extra]
    pos = 2 + n_w + n_extra
    if has_rider:
        xr_ref = refs[pos]
        pos += 1
    o_refs = refs[pos:pos + n_out]
    i, j = pl.program_id(0), pl.program_id(1)

    @pl.when(j == 0)
    def _():
        refs[-1][...] = _rmsnorm_bf16(x_ref[...], g_ref[...])

    xn = refs[-1][...]
    w_tiles = [w[...].astype(BF16) for w in w_refs]
    epi([jnp.dot(xn, w, preferred_element_type=F32) for w in w_tiles], e_refs, o_refs)

    if has_rider:
        ro_refs = refs[pos + n_out:pos + 2 * n_out]
        xrn_ref = refs[-2]

        @pl.when(i == 0)
        def _():
            @pl.when(j == 0)
            def _():
                xrn_ref[...] = _rmsnorm_bf16(xr_ref[...], g_ref[...])

            xrn = xrn_ref[...]
            epi([jnp.dot(xrn, w, preferred_element_type=F32) for w in w_tiles], e_refs, [o.at[0] for o in ro_refs])

        @pl.when(i != 0)
        def _():
            for o in ro_refs:
                o[...] = jnp.zeros(o.shape, o.dtype)


def _norm_matmul(x, g, ws, extras, outs, epi, *, tm, tn, name, seq_len=None, n=None, col_starts=None,
                 rider=None):
    m, d = x.shape
    n = _weight_cols(ws[0]) if n is None else n
    col_starts = [0] * len(ws) if col_starts is None else col_starts
    tm = min(tm, m)
    grid = (m // tm, n // tn)
    in_specs = [pl.BlockSpec((tm, d), lambda i, j: (i, 0)),
                pl.BlockSpec((1, d), lambda i, j: (0, 0))]
    w_arrays = []
    for w, c in zip(ws, col_starts):
        arr, spec = _weight_operand(w, d, tn, c // tn)
        w_arrays.append(arr)
        in_specs.append(spec)
    in_specs += [pl.BlockSpec((tm, LANES), lambda i, j: (i, 0)) for _ in extras]
    out_specs, out_shape = [], []
    for dt, transposed in outs:
        if transposed:
            per_seq = seq_len // tm
            out_specs.append(pl.BlockSpec((1, tn, tm), lambda i, j: (i // per_seq, j, i % per_seq)))
            out_shape.append(jax.ShapeDtypeStruct((m // seq_len, n, seq_len), dt))
        else:
            out_specs.append(pl.BlockSpec((tm, tn), lambda i, j: (i, j)))
            out_shape.append(jax.ShapeDtypeStruct((m, n), dt))
    scratch = [pltpu.VMEM((tm, d), BF16)]
    operands = [x, g.reshape(1, d), *w_arrays, *extras]
    if rider is not None:
        assert not extras and not any(transposed for _, transposed in outs)
        mr = rider.shape[0]
        in_specs.append(pl.BlockSpec((mr, d), lambda i, j: (0, 0)))
        operands.append(rider)
        for dt, _ in outs:
            out_specs.append(pl.BlockSpec((1, mr, tn), lambda i, j: (i, 0, j)))
            out_shape.append(jax.ShapeDtypeStruct((grid[0], mr, n), dt))
        scratch.insert(0, pltpu.VMEM((mr, d), BF16))
    kern = functools.partial(_nmm_kernel, n_w=len(ws), n_extra=len(extras), n_out=len(outs), epi=epi,
                             has_rider=rider is not None)
    res = pl.pallas_call(
        kern, grid=grid, in_specs=in_specs, out_specs=out_specs, out_shape=out_shape,
        scratch_shapes=scratch, compiler_params=_cparams(("arbitrary", "arbitrary")), name=name,
    )(*operands)
    if rider is not None:
        res = list(res[:len(outs)]) + [r[0] for r in res[len(outs):]]
    return res


def _epi_recurrent_in(accs, e_refs, o_refs):
    o_refs[0][...] = jax.nn.gelu(accs[0])
    o_refs[1][...] = accs[1]


def _epi_swiglu(accs, e_refs, o_refs):
    o_refs[0][...] = (jax.nn.silu(accs[0]) * accs[1]).astype(BF16)


def _rope_slab(y, cos, sin_signed):
    lane = lax.broadcasted_iota(jnp.int32, y.shape, 1)
    first_half = (lane % 64) < 32
    rot = jnp.where(first_half, pltpu.roll(y, LANES - 32, 1), pltpu.roll(y, 32, 1))
    return y * cos + rot * sin_signed


def _epi_q(accs, e_refs, o_refs, *, scale, transposed):
    cos, sin_signed = e_refs[0][...], e_refs[1][...]
    for s in range(accs[0].shape[1] // LANES):
        sl = slice(s * LANES, (s + 1) * LANES)
        q = _rope_slab(accs[0][:, sl], cos, sin_signed) * scale
        if transposed:
            o_refs[0][0, sl, :] = q.T.astype(BF16)
        else:
            o_refs[0][:, sl] = q.astype(BF16)


def _epi_kv(accs, e_refs, o_refs):
    cos, sin_signed = e_refs[0][...], e_refs[1][...]
    for s in range(accs[0].shape[1] // LANES):
        sl = slice(s * LANES, (s + 1) * LANES)
        k = _rope_slab(accs[0][:, sl], cos, sin_signed)
        o_refs[0][:, sl] = k
        o_refs[1][:, sl] = k.astype(BF16)
    o_refs[2][...] = accs[1]
    o_refs[3][...] = accs[1].astype(BF16)


def _epi_kv_transposed(accs, e_refs, o_refs):
    cos, sin_signed = e_refs[0][...], e_refs[1][...]
    for s in range(accs[0].shape[1] // LANES):
        sl = slice(s * LANES, (s + 1) * LANES)
        k = _rope_slab(accs[0][:, sl], cos, sin_signed)
        o_refs[0][0, sl, :] = k.T
        o_refs[1][:, sl] = k.astype(BF16)
        o_refs[3][0, sl, :] = accs[1][:, sl].T.astype(BF16)
    o_refs[2][...] = accs[1]


def _mm_res_kernel(a_ref, w_ref, r_ref, ar_ref, rr_ref, o_ref, or_ref):
    w = w_ref[...].astype(BF16)
    o_ref[...] = r_ref[...] + jnp.dot(a_ref[...], w, preferred_element_type=F32)
    i = pl.program_id(0)

    @pl.when(i == 0)
    def _():
        or_ref[0] = rr_ref[...] + jnp.dot(ar_ref[...], w, preferred_element_type=F32)

    @pl.when(i != 0)
    def _():
        or_ref[...] = jnp.zeros(or_ref.shape, or_ref.dtype)


def _matmul_residual(a, w, res, a_rider, res_rider, *, tm, tn, name):
    m, k = a.shape
    mr = a_rider.shape[0]
    n = _weight_cols(w)
    w, w_spec = _weight_operand(w, k, tn)
    tm = min(tm, m)
    out, out_rider = pl.pallas_call(
        _mm_res_kernel, grid=(m // tm, n // tn),
        in_specs=[pl.BlockSpec((tm, k), lambda i, j: (i, 0)),
                  w_spec,
                  pl.BlockSpec((tm, tn), lambda i, j: (i, j)),
                  pl.BlockSpec((mr, k), lambda i, j: (0, 0)),
                  pl.BlockSpec((mr, tn), lambda i, j: (0, j))],
        out_specs=[pl.BlockSpec((tm, tn), lambda i, j: (i, j)),
                   pl.BlockSpec((1, mr, tn), lambda i, j: (i, 0, j))],
        out_shape=[jax.ShapeDtypeStruct((m, n), F32), jax.ShapeDtypeStruct((m // tm, mr, n), F32)],
        compiler_params=_cparams(("arbitrary", "arbitrary")), name=name,
    )(a, w, res, a_rider, res_rider)
    return out, out_rider[0]


def _rmsnorm_kernel(x_ref, g_ref, o_ref):
    x = x_ref[...]
    ms = jnp.mean(x * x, axis=-1, keepdims=True)
    o_ref[...] = (x * lax.rsqrt(ms + EPS)) * g_ref[...]


def _rmsnorm(x, g, *, tm):
    m, d = x.shape
    tm = min(tm, m)
    return pl.pallas_call(
        _rmsnorm_kernel, grid=(m // tm,),
        in_specs=[pl.BlockSpec((tm, d), lambda i: (i, 0)), pl.BlockSpec((1, d), lambda i: (0, 0))],
        out_specs=pl.BlockSpec((tm, d), lambda i: (i, 0)),
        out_shape=jax.ShapeDtypeStruct((m, d), F32),
        compiler_params=_cparams(("parallel",)), name="final_rmsnorm",
    )(x, g.reshape(1, d))


_EXPM1_SERIES_BOUND = 2.0 ** -6
_EXPM1_SERIES_TERMS = 4


def _neg_expm1_twice(half_x, exp_half_x):
    x = 2.0 * half_x
    poly = jnp.full_like(x, 1.0 / math.factorial(_EXPM1_SERIES_TERMS))
    for k in range(_EXPM1_SERIES_TERMS - 1, 0, -1):
        poly = poly * x + 1.0 / math.factorial(k)
    return jnp.where(x > -_EXPM1_SERIES_BOUND, -(poly * x), 1.0 - exp_half_x * exp_half_x)


def _scan_kernel(u_ref, gate_ref, cw_ref, cb_ref, wa_ref, ba_ref, wi_ref, bi_ref, lam_ref, conv0_ref, h0_ref,
                 hg_ref, convout_ref, hlast_ref, ubuf, a_s, b_s, hs, hcar, *, tc, last_row, n_gate_blocks):
    c = pl.program_id(1)
    width = u_ref.shape[2]
    pad = SUBLANES
    hist = CONV_WIDTH - 1

    @pl.when(c == 0)
    def _():
        ubuf[0:pad, :] = jnp.zeros((pad, width), F32)
        ubuf[pad - hist:pad, :] = conv0_ref[0]
        hcar[...] = jnp.broadcast_to(h0_ref[0], (SUBLANES, width))

    u = u_ref[0]
    ubuf[pad:pad + tc, :] = u
    cw = cw_ref[...]
    xc = cb_ref[...] + u * cw[0:1, :]
    for j in range(1, CONV_WIDTH):
        xc = xc + ubuf[pad - j:pad - j + tc, :] * cw[j:j + 1, :]

    neg_c_softplus = -RG_C * jax.nn.softplus(-lam_ref[...])
    gb = width // n_gate_blocks
    for n in range(n_gate_blocks):
        sl = slice(n * gb, (n + 1) * gb)
        xs = xc[:, sl]
        xb = xs.astype(BF16)
        r = jax.nn.sigmoid(jnp.dot(xb, wa_ref[n].astype(BF16), preferred_element_type=F32) + ba_ref[:, sl])
        i = jax.nn.sigmoid(jnp.dot(xb, wi_ref[n].astype(BF16), preferred_element_type=F32) + bi_ref[:, sl])
        log_a = r * neg_c_softplus[:, sl]
        a = jnp.exp(log_a)
        a_s[:, sl] = a
        b_s[:, sl] = jnp.sqrt(_neg_expm1_twice(log_a, a)) * i * xs

    row = lax.broadcasted_iota(jnp.int32, (SUBLANES, width), 0)

    def group(g, h):
        base = pl.multiple_of(g * SUBLANES, SUBLANES)
        a8 = a_s[pl.ds(base, SUBLANES), :]
        b8 = b_s[pl.ds(base, SUBLANES), :]
        out = jnp.zeros((SUBLANES, width), F32)
        for r in range(SUBLANES):
            hn = a8 * h + b8
            out = jnp.where(row == r, hn, out)
            h = jnp.broadcast_to(hn[r:r + 1, :], (SUBLANES, width))
        hs[pl.ds(base, SUBLANES), :] = out
        return h

    hcar[...] = lax.fori_loop(0, tc // SUBLANES, group, hcar[...])
    hg_ref[0] = (hs[...] * gate_ref[0]).astype(BF16)
    ubuf[0:pad, :] = ubuf[tc:tc + pad, :]

    @pl.when(c == pl.num_programs(1) - 1)
    def _():
        convout_ref[0] = ubuf[pad + last_row + 1 - hist:pad + last_row + 1, :]
        hlast_ref[0] = hs[last_row:last_row + 1, :]


def _conv_rglru(u, gate, conv0, h0, cw, cb, wa, ba, wi, bi, lam, *, tc, t_valid):
    b, t, width = u.shape
    tc = min(tc, t)
    n_chunks = t // tc
    last_row = (t_valid - 1) - (n_chunks - 1) * tc
    nb = wa.shape[0]
    row2 = lambda v: v.reshape(1, width)
    kern = functools.partial(_scan_kernel, tc=tc, last_row=last_row, n_gate_blocks=nb)
    full2 = lambda shape: pl.BlockSpec(shape, lambda i, c: (0, 0))
    return pl.pallas_call(
        kern, grid=(b, n_chunks),
        in_specs=[pl.BlockSpec((1, tc, width), lambda i, c: (i, c, 0)),
                  pl.BlockSpec((1, tc, width), lambda i, c: (i, c, 0)),
                  full2((CONV_WIDTH, width)), full2((1, width)),
                  pl.BlockSpec(wa.shape, lambda i, c: (0, 0, 0)), full2((1, width)),
                  pl.BlockSpec(wi.shape, lambda i, c: (0, 0, 0)), full2((1, width)),
                  full2((1, width)),
                  pl.BlockSpec((1, CONV_WIDTH - 1, width), lambda i, c: (i, 0, 0)),
                  pl.BlockSpec((1, 1, width), lambda i, c: (i, 0, 0))],
        out_specs=[pl.BlockSpec((1, tc, width), lambda i, c: (i, c, 0)),
                   pl.BlockSpec((1, CONV_WIDTH - 1, width), lambda i, c: (i, 0, 0)),
                   pl.BlockSpec((1, 1, width), lambda i, c: (i, 0, 0))],
        out_shape=[jax.ShapeDtypeStruct((b, t, width), BF16),
                   jax.ShapeDtypeStruct((b, CONV_WIDTH - 1, width), F32),
                   jax.ShapeDtypeStruct((b, 1, width), F32)],
        scratch_shapes=[pltpu.VMEM((tc + SUBLANES, width), F32),
                        pltpu.VMEM((tc, width), F32), pltpu.VMEM((tc, width), F32),
                        pltpu.VMEM((tc, width), F32), pltpu.VMEM((SUBLANES, width), F32)],
        compiler_params=_cparams(("parallel", "arbitrary")), name="conv_rglru",
    )(u, gate, cw, row2(cb), wa, row2(ba), wi, row2(bi), row2(lam), conv0, h0.reshape(b, 1, width))


def _diff_lambda(lq1_ref, lk1_ref, lq2_ref, lk2_ref, lam_init):
    e1 = jnp.exp(jnp.sum(lq1_ref[...] * lk1_ref[...], axis=-1, keepdims=True))
    e2 = jnp.exp(jnp.sum(lq2_ref[...] * lk2_ref[...], axis=-1, keepdims=True))
    return e1 - e2 + lam_init


LOG2_E = math.log2(math.e)


def _online_softmax_step(s, v, m_ref, l_ref, acc_ref):
    m_old = m_ref[...]
    m_new = jnp.maximum(m_old, jnp.max(s, axis=-1, keepdims=True))
    alpha = jnp.exp2(m_old - m_new)
    p = jnp.exp2(s - m_new)
    l_ref[...] = alpha * l_ref[...] + jnp.sum(p, axis=-1, keepdims=True)
    acc_ref[...] = alpha * acc_ref[...] + jnp.dot(p.astype(BF16), v, preferred_element_type=F32)
    m_ref[...] = m_new


def _head_out(n0, n1, lam, subln, out_scale):
    o = n0 - lam * n1
    ms = jnp.mean(o * o, axis=-1, keepdims=True)
    return ((o * lax.rsqrt(ms + EPS)) * subln) * out_scale


SUM_ROWS = 16


def _attn_prompt_kernel(qidx_ref, kidx_ref, lq1_ref, lk1_ref, lq2_ref, lk2_ref, subln_ref, bias_ref,
                        qt_ref, k_ref, vt_ref, o_ref, qs_ref, vta_ref, s_ref, cmax_ref, m_ref, acc_ref, *,
                        tq, n_items, lam_init):
    n_q = qt_ref.shape[2] // tq
    feat = lax.broadcasted_iota(jnp.int32, (LANES, tq), 0)
    zero = jnp.zeros((LANES, tq), BF16)
    half = LANES // 2
    for qi in range(n_q):
        qt = qt_ref[0, :, qi * tq:(qi + 1) * tq]
        qs_ref[:, (2 * qi) * tq:(2 * qi + 1) * tq] = jnp.where(feat < half, qt, zero)
        qs_ref[:, (2 * qi + 1) * tq:(2 * qi + 2) * tq] = jnp.where(feat >= half, qt, zero)
    m_ref[...] = jnp.full(m_ref.shape, NEG_BIG, F32)
    acc_ref[...] = jnp.zeros(acc_ref.shape, F32)
    vta_ref[0:LANES, :] = vt_ref[0]
    vta_ref[LANES:, :] = jnp.ones((vta_ref.shape[0] - LANES, vta_ref.shape[1]), BF16)
    lam = _diff_lambda(lq1_ref, lk1_ref, lq2_ref, lk2_ref, lam_init)

    def produce(w, slot):
        qi, ki = qidx_ref[w], kidx_ref[w]
        kstart = pl.multiple_of(ki * tq, tq)
        qstart = pl.multiple_of(qi * (2 * tq), 2 * tq)
        s = jnp.dot(k_ref[0, pl.ds(kstart, tq), :], qs_ref[:, pl.ds(qstart, 2 * tq)],
                    preferred_element_type=F32)
        s = s + bias_ref[(qi == ki).astype(jnp.int32)]
        s_ref[slot] = s
        cmax_ref[slot] = jnp.max(s, axis=0, keepdims=True)

    def consume(w, slot):
        qi, ki = qidx_ref[w], kidx_ref[w]
        kstart = pl.multiple_of(ki * tq, tq)
        m_old = jnp.where(ki == 0, NEG_BIG, m_ref[...])
        m_new = jnp.maximum(m_old, cmax_ref[slot])
        alpha = jnp.exp2(m_old - m_new)
        p = jnp.exp2(s_ref[slot] - m_new)
        pv = jnp.dot(vta_ref[:, pl.ds(kstart, tq)], p.astype(BF16), preferred_element_type=F32)
        acc_ref[qi] = alpha * acc_ref[qi] + pv
        m_ref[...] = m_new

    produce(0, 0)
    per_trip = 6
    n_looped = max(n_items // per_trip - 1, 0)

    def trip(i, carry):
        for g in range(per_trip):
            produce(per_trip * i + g + 1, (g + 1) % 2)
            consume(per_trip * i + g, g % 2)
        return carry

    lax.fori_loop(0, n_looped, trip, 0)
    for w in range(per_trip * n_looped, n_items):
        if w + 1 < n_items:
            produce(w + 1, (w + 1) % 2)
        consume(w, w % 2)

    for qi in range(n_q):
        acc = acc_ref[qi]
        n = acc[0:LANES] * (1.0 / acc[LANES:LANES + 1])
        o = n[:, 0:tq] - lam * n[:, tq:2 * tq]
        ms = jnp.mean(o * o, axis=0, keepdims=True)
        y = ((o * lax.rsqrt(ms + EPS)) * subln_ref[...]) * (1.0 - lam_init)
        o_ref[0, qi * tq:(qi + 1) * tq, :] = y.T.astype(BF16)


def _attn_prompt(qt, k, vt, lq1, lk1, lq2, lk2, subln, lam_init, *, tq):
    b, hw, t = qt.shape
    nh = hw // LANES
    assert t % tq == 0
    n_q = t // tq
    items = [(qi, ki) for qi in range(n_q) for ki in range(qi + 1)]
    qidx = jnp.array([qi for qi, _ in items], jnp.int32)
    kidx = jnp.array([ki for _, ki in items], jnp.int32)
    kpos = lax.broadcasted_iota(jnp.int32, (tq, 2 * tq), 0)
    qpos = lax.broadcasted_iota(jnp.int32, (tq, 2 * tq), 1) % tq
    bias = jnp.stack([jnp.zeros((tq, 2 * tq), F32), jnp.where(kpos <= qpos, 0.0, NEG_BIG).astype(F32)])
    vec = lambda a: a.reshape(1, -1)
    small = lambda w: pl.BlockSpec((1, w), lambda i, h, *_: (0, 0))
    kern = functools.partial(_attn_prompt_kernel, tq=tq, n_items=len(items), lam_init=lam_init)
    grid_spec = pltpu.PrefetchScalarGridSpec(
        num_scalar_prefetch=2, grid=(b, nh),
        in_specs=[small(lq1.size), small(lk1.size), small(lq2.size), small(lk2.size),
                  pl.BlockSpec((LANES, 1), lambda i, h, *_: (0, 0)),
                  pl.BlockSpec((2, tq, 2 * tq), lambda i, h, *_: (0, 0, 0)),
                  pl.BlockSpec((1, LANES, t), lambda i, h, *_: (i, h, 0)),
                  pl.BlockSpec((1, t, LANES), lambda i, h, *_: (i, 0, h)),
                  pl.BlockSpec((1, LANES, t), lambda i, h, *_: (i, h, 0))],
        out_specs=pl.BlockSpec((1, t, LANES), lambda i, h, *_: (i, 0, h)),
        scratch_shapes=[pltpu.VMEM((LANES, 2 * t), BF16), pltpu.VMEM((LANES + SUM_ROWS, t), BF16),
                        pltpu.VMEM((2, tq, 2 * tq), F32), pltpu.VMEM((2, 1, 2 * tq), F32),
                        pltpu.VMEM((1, 2 * tq), F32), pltpu.VMEM((n_q, LANES + SUM_ROWS, 2 * tq), F32)])
    return pl.pallas_call(
        kern, grid_spec=grid_spec, out_shape=jax.ShapeDtypeStruct((b, t, hw), BF16),
        compiler_params=_cparams(("parallel", "parallel")), name="diff_attn_prompt",
    )(qidx, kidx, vec(lq1), vec(lk1), vec(lq2), vec(lk2), subln.reshape(LANES, 1), bias, qt, k, vt)


DEC_ROWS = 2 * SUBLANES


PAGE_RING = 3


def _page_copies(pt_ref, kc_hbm, vc_hbm, kbuf, vbuf, sem, step, pages):
    n_j = pl.num_programs(1)
    b, j = step // n_j, step % n_j
    slot = step % PAGE_RING
    copies = []
    for p in range(pages):
        pid = pt_ref[b, j * pages + p]
        copies.append(pltpu.make_async_copy(kc_hbm.at[pid], kbuf.at[slot, p], sem.at[slot]))
        copies.append(pltpu.make_async_copy(vc_hbm.at[pid], vbuf.at[slot, p], sem.at[slot]))
    return copies


def _attn_decode_kernel(pt_ref, lq1_ref, lk1_ref, lq2_ref, lk2_ref, subln_ref, q_ref, kn_ref, vn_ref,
                        kc_hbm, vc_hbm, o_ref, m_ref, l_ref, acc_ref, vh_ref, kbuf, vbuf, sem, *,
                        pages, n_heads, t_new, lam_init):
    j = pl.program_id(1)
    n_steps = pl.num_programs(0) * pl.num_programs(1)
    step = pl.program_id(0) * pl.num_programs(1) + j
    copies_of = functools.partial(_page_copies, pt_ref, kc_hbm, vc_hbm, kbuf, vbuf, sem, pages=pages)

    @pl.when(step == 0)
    def _():
        for ahead in range(PAGE_RING - 1):
            @pl.when(ahead < n_steps)
            def _():
                for c in copies_of(jnp.int32(ahead)):
                    c.start()

    @pl.when(step + (PAGE_RING - 1) < n_steps)
    def _():
        for c in copies_of(step + (PAGE_RING - 1)):
            c.start()

    for c in copies_of(step):
        c.wait()
    slot = step % PAGE_RING

    @pl.when(j == 0)
    def _():
        m_ref[...] = jnp.full(m_ref.shape, NEG_BIG, F32)
        l_ref[...] = jnp.zeros(l_ref.shape, F32)
        acc_ref[...] = jnp.zeros(acc_ref.shape, F32)

    page = kbuf.shape[3]
    for p in range(pages):
        v_heads = jnp.swapaxes(vbuf[slot, p].reshape(page, n_heads, LANES), 0, 1)
        vh_ref[:, p * page:(p + 1) * page, :] = v_heads.astype(BF16)
    head_lanes = lambda h: slice(h * LANES, (h + 1) * LANES)
    head_rows = lambda h: slice(h * DEC_ROWS, (h + 1) * DEC_ROWS)

    def softmax_step(s_heads, v_of_head):
        s = jnp.concatenate(s_heads, axis=0)
        m_old = m_ref[...]
        m_new = jnp.maximum(m_old, jnp.max(s, axis=-1, keepdims=True))
        alpha = jnp.exp2(m_old - m_new)
        p = jnp.exp2(s - m_new)
        l_ref[...] = alpha * l_ref[...] + jnp.sum(p, axis=-1, keepdims=True)
        pb = p.astype(BF16)
        pv = [jnp.dot(pb[head_rows(h)], v_of_head(h), preferred_element_type=F32) for h in range(n_heads)]
        acc_ref[...] = alpha * acc_ref[...] + jnp.concatenate(pv, axis=0)
        m_ref[...] = m_new

    s_heads = [jnp.dot(q_ref[0, h],
                       jnp.concatenate([kbuf[slot, p, head_lanes(h), :].astype(BF16) for p in range(pages)], axis=1),
                       preferred_element_type=F32) for h in range(n_heads)]
    softmax_step(s_heads, lambda h: vh_ref[h])

    @pl.when(j == pl.num_programs(1) - 1)
    def _():
        lam = _diff_lambda(lq1_ref, lk1_ref, lq2_ref, lk2_ref, lam_init)
        n_new = kn_ref.shape[1]
        qt = lax.broadcasted_iota(jnp.int32, (DEC_ROWS, n_new), 0) % SUBLANES
        kt_pos = lax.broadcasted_iota(jnp.int32, (DEC_ROWS, n_new), 1)
        visible = (kt_pos <= qt) & (kt_pos < t_new)
        s_new = [jnp.where(visible,
                           lax.dot_general(q_ref[0, h], kn_ref[0, :, head_lanes(h)], (((1,), (1,)), ((), ())),
                                           preferred_element_type=F32), NEG_BIG) for h in range(n_heads)]
        softmax_step(s_new, lambda h: vn_ref[0, :, head_lanes(h)])
        n = acc_ref[...] / l_ref[...]
        for h in range(n_heads):
            nh_ = n[head_rows(h)]
            o_ref[0, :, head_lanes(h)] = _head_out(nh_[0:SUBLANES], nh_[SUBLANES:DEC_ROWS], lam, subln_ref[...],
                                                   1.0 - lam_init)


def _attn_decode(qd, k_new, v_new, cache_k, cache_v, page_table, lq1, lk1, lq2, lk2, subln, lam_init, *,
                 pages, t_new):
    b, nh = qd.shape[0], qd.shape[1]
    hw = nh * LANES
    n_pages = page_table.shape[1]
    assert n_pages % pages == 0
    vec = lambda a: a.reshape(1, -1)
    small = lambda w: pl.BlockSpec((1, w), lambda i, j, pt: (0, 0))

    kern = functools.partial(_attn_decode_kernel, pages=pages, n_heads=nh, t_new=t_new, lam_init=lam_init)
    grid_spec = pltpu.PrefetchScalarGridSpec(
        num_scalar_prefetch=1, grid=(b, n_pages // pages),
        in_specs=[small(lq1.size), small(lk1.size), small(lq2.size), small(lk2.size), small(LANES),
                  pl.BlockSpec((1, nh, DEC_ROWS, LANES), lambda i, j, pt: (i, 0, 0, 0)),
                  pl.BlockSpec((1, k_new.shape[1], hw), lambda i, j, pt: (i, 0, 0)),
                  pl.BlockSpec((1, v_new.shape[1], hw), lambda i, j, pt: (i, 0, 0)),
                  pl.BlockSpec(memory_space=pl.ANY), pl.BlockSpec(memory_space=pl.ANY)],
        out_specs=pl.BlockSpec((1, SUBLANES, hw), lambda i, j, pt: (i, 0, 0)),
        scratch_shapes=[pltpu.VMEM((nh * DEC_ROWS, 1), F32), pltpu.VMEM((nh * DEC_ROWS, 1), F32),
                        pltpu.VMEM((nh * DEC_ROWS, LANES), F32),
                        pltpu.VMEM((nh, pages * cache_k.shape[2], LANES), BF16),
                        pltpu.VMEM((PAGE_RING, pages) + cache_k.shape[1:], F32),
                        pltpu.VMEM((PAGE_RING, pages) + cache_v.shape[1:], F32),
                        pltpu.SemaphoreType.DMA((PAGE_RING,))])
    return pl.pallas_call(
        kern, grid_spec=grid_spec, out_shape=jax.ShapeDtypeStruct((b, SUBLANES, hw), F32),
        compiler_params=_cparams(("arbitrary", "arbitrary")), name="diff_attn_decode",
    )(page_table, vec(lq1), vec(lk1), vec(lq2), vec(lk2), vec(subln), qd, k_new, v_new, cache_k, cache_v)


def _rope_tables(pos, head_dim, batch):
    half = head_dim // 2
    inv = ROPE_THETA ** (-jnp.arange(half, dtype=F32) / half)
    ang = pos.astype(F32)[:, None] * inv[None, :]
    cos, sin = jnp.cos(ang), jnp.sin(ang)
    reps = LANES // head_dim
    cos_t = jnp.tile(jnp.concatenate([cos, cos], axis=1), (batch, reps))
    sin_t = jnp.tile(jnp.concatenate([-sin, sin], axis=1), (batch, reps))
    return cos_t, sin_t


def _plain(*dtypes):
    return [(dt, False) for dt in dtypes]


class _Group:
    def __init__(self, x, pos, conv0, h0, decode, cfg, head_dim):
        self.b, self.t, d = x.shape
        self.m = self.b * self.t
        self.x = x.reshape(self.m, d)
        self.cos, self.sin = _rope_tables(pos, head_dim, self.b)
        self.conv0, self.h0, self.decode, self.cfg = conv0, h0, decode, cfg
        self.prompt = decode is None
        self.new_bufs, self.new_hs = [], []
        self.kb = self.vb = self.k_out = self.v_new = None

    def mixer_recurrent(self, gate, u, l, p):
        b, t = self.b, self.t
        width = u.shape[1]
        t_pad = -(-t // SUBLANES) * SUBLANES
        u3, g3 = u.reshape(b, t, width), gate.reshape(b, t, width)
        if t_pad != t:
            u3 = jnp.pad(u3, ((0, 0), (0, t_pad - t), (0, 0)))
            g3 = jnp.pad(g3, ((0, 0), (0, t_pad - t), (0, 0)))
        hg, nb, nh = _conv_rglru(u3, g3, self.conv0[l], self.h0[l], p['rg_conv_w'][l], p['rg_conv_b'][l],
                                 p['rg_w_a'][l], p['rg_b_a'][l], p['rg_w_i'][l], p['rg_b_i'][l],
                                 p['rg_lambda'][l], tc=self.cfg['tc'], t_valid=t)
        self.new_bufs.append(nb)
        self.new_hs.append(nh.reshape(b, width))
        return hg[:, :t].reshape(self.m, width)

    def shared_kv(self, p, n_heads, head_dim):
        b, t, cfg = self.b, self.t, self.cfg
        qk_width = n_heads * 2 * head_dim
        assert p['w_kv'].shape[1] == 2 * qk_width
        kv_ws = dict(ws=[p['w_kv'], p['w_kv']], n=qk_width, col_starts=[0, qk_width])
        common = dict(extras=[self.cos, self.sin], tm=cfg['tm'], tn=cfg['tn'], name="shared_kv", **kv_ws)
        if self.prompt:
            k_t, self.kb, self.v_new, self.vb = _norm_matmul(
                self.x, p['kv_norm'], outs=[(F32, True), (BF16, False), (F32, False), (BF16, True)],
                epi=_epi_kv_transposed, seq_len=t, **common)
            self.k_out = k_t.reshape(b, n_heads, 2, head_dim, t).transpose(0, 4, 1, 2, 3)
        else:
            k_new, self.kb, self.v_new, self.vb = _norm_matmul(
                self.x, p['kv_norm'], outs=_plain(F32, BF16, F32, BF16), epi=_epi_kv, **common)
            self.k_out = k_new.reshape(b, t, n_heads, 2, head_dim)

    def mixer_attention(self, l, j, p, n_heads, head_dim):
        b, t, m, cfg = self.b, self.t, self.m, self.cfg
        lam_init = 0.8 - 0.6 * math.exp(-0.3 * l)
        (q,) = _norm_matmul(self.x, p['norm_mix'][l], [(p['dif_w_q'], j)], [self.cos, self.sin],
                            [(BF16, self.prompt)],
                            functools.partial(_epi_q, scale=head_dim ** -0.5 * LOG2_E, transposed=self.prompt),
                            tm=cfg['tm'], tn=cfg['tn'], name="q_proj", seq_len=t)
        lams = (p['dif_lq1'][j], p['dif_lk1'][j], p['dif_lq2'][j], p['dif_lk2'][j], p['dif_subln'][j])
        if self.prompt:
            o = _attn_prompt(q, self.kb.reshape(b, t, -1), self.vb, *lams, lam_init, tq=cfg['tq'])
            return o.reshape(m, -1)
        cache_k, cache_v, page_table = self.decode
        q4 = q.reshape(b, t, n_heads, LANES).transpose(0, 2, 1, 3)
        lane = jnp.arange(LANES)
        zpad = jnp.zeros((b, n_heads, SUBLANES - t, LANES), BF16)
        qd = jnp.concatenate([jnp.where(lane < head_dim, q4, 0).astype(BF16), zpad,
                              jnp.where(lane >= head_dim, q4, 0).astype(BF16), zpad], axis=2)
        pad_new = lambda a: jnp.pad(a.reshape(b, t, -1), ((0, 0), (0, DEC_ROWS - t), (0, 0)))
        o = _attn_decode(qd, pad_new(self.kb), pad_new(self.vb), cache_k, cache_v, page_table, *lams, lam_init,
                         pages=cfg['pages'], t_new=t)
        return o[:, :t].reshape(m, -1).astype(BF16)

    def outputs(self, p, n_heads):
        d = self.x.shape[1]
        y = _rmsnorm(self.x, p['norm_final'], tm=self.cfg['tm_norm']).reshape(self.b, self.t, d)
        v_out = self.v_new.reshape(self.b, self.t, n_heads, -1)
        return y, self.k_out, v_out, jnp.stack(self.new_bufs), jnp.stack(self.new_hs)


def _trunks(main, rider, p):
    depth = p['norm_mix'].shape[0]
    n_a = p['rg_w_x'].shape[0]
    head_dim = p['dif_lq1'].shape[1]
    n_heads = p['dif_w_q'].shape[2] // (2 * head_dim)
    cfg = main.cfg
    tm, tn = cfg['tm'], cfg['tn']
    for l in range(depth):
        if l < n_a:
            gate, u, gate_r, u_r = _norm_matmul(
                main.x, p['norm_mix'][l], [(p['rg_w_gate'], l), (p['rg_w_x'], l)], [], _plain(F32, F32),
                _epi_recurrent_in, tm=tm, tn=tn, name="recurrent_in", rider=rider.x)
            mixed, mixed_r = main.mixer_recurrent(gate, u, l, p), rider.mixer_recurrent(gate_r, u_r, l, p)
            w_out = (p['rg_w_out'], l)
        else:
            j = l - n_a
            if j == 0:
                main.shared_kv(p, n_heads, head_dim)
                rider.shared_kv(p, n_heads, head_dim)
            mixed = main.mixer_attention(l, j, p, n_heads, head_dim)
            mixed_r = rider.mixer_attention(l, j, p, n_heads, head_dim)
            w_out = (p['dif_w_o'], j)
        main.x, rider.x = _matmul_residual(mixed, w_out, main.x, mixed_r, rider.x, tm=cfg['tm_out'], tn=tn,
                                           name="mixer_out")
        hmid, hmid_r = _norm_matmul(main.x, p['norm_ffn'][l], [(p['ffn_w_gate'], l), (p['ffn_w_up'], l)], [],
                                    _plain(BF16), _epi_swiglu, tm=tm, tn=tn, name="ffn_in", rider=rider.x)
        main.x, rider.x = _matmul_residual(hmid, (p['ffn_w_down'], l), main.x, hmid_r, rider.x,
                                           tm=cfg['tm_down'], tn=tn, name="ffn_down")
    return main.outputs(p, n_heads), rider.outputs(p, n_heads)


def kernel(x_prompt, x_sample, cache_k, cache_v, page_table, state_conv, state_rglru, norm_mix, norm_ffn, norm_final, rg_w_x, rg_w_gate, rg_conv_w, rg_conv_b, rg_w_a, rg_b_a, rg_w_i, rg_b_i, rg_lambda, rg_w_out, kv_norm, w_kv, dif_w_q, dif_lq1, dif_lk1, dif_lq2, dif_lk2, dif_subln, dif_w_o, ffn_w_gate, ffn_w_up, ffn_w_down):
    p = dict(norm_mix=norm_mix, norm_ffn=norm_ffn, norm_final=norm_final, rg_w_x=rg_w_x,
             rg_w_gate=rg_w_gate, rg_conv_w=rg_conv_w, rg_conv_b=rg_conv_b, rg_w_a=rg_w_a,
             rg_b_a=rg_b_a, rg_w_i=rg_w_i, rg_b_i=rg_b_i, rg_lambda=rg_lambda, rg_w_out=rg_w_out,
             kv_norm=kv_norm, w_kv=w_kv, dif_w_q=dif_w_q, dif_lq1=dif_lq1, dif_lk1=dif_lk1,
             dif_lq2=dif_lq2, dif_lk2=dif_lk2, dif_subln=dif_subln, dif_w_o=dif_w_o,
             ffn_w_gate=ffn_w_gate, ffn_w_up=ffn_w_up, ffn_w_down=ffn_w_down)
    p['ffn_w_down'] = ffn_w_down.astype(BF16)

    n_a = rg_w_x.shape[0]
    width = rg_w_x.shape[2]
    head_dim = dif_lq1.shape[1]
    b_p, t_p = x_prompt.shape[0], x_prompt.shape[1]
    cfg_p = dict(tm=1024, tn=512, tm_out=2048, tm_down=1024, tm_norm=512, tc=256, tq=512)
    conv0 = jnp.zeros((n_a, b_p, CONV_WIDTH - 1, width), F32)
    h0 = jnp.zeros((n_a, b_p, width), F32)
    prompt = _Group(x_prompt, jnp.arange(t_p, dtype=jnp.int32), conv0, h0, None, cfg_p, head_dim)

    b_s, t_s = x_sample.shape[0], x_sample.shape[1]
    n_phys, page = cache_k.shape[0], cache_k.shape[1]
    past_len = page_table.shape[1] * page
    cfg_s = dict(tm=b_s * t_s, tn=512, tm_norm=b_s * t_s, tc=SUBLANES, pages=4)
    k_pages = jnp.transpose(cache_k, (0, 2, 3, 4, 1)).reshape(n_phys, -1, page)
    v_pages = cache_v.reshape(n_phys, page * cache_v.shape[2], cache_v.shape[3])
    pos_s = past_len + jnp.arange(t_s, dtype=jnp.int32)
    sample = _Group(x_sample, pos_s, state_conv, state_rglru, (k_pages, v_pages, page_table), cfg_s, head_dim)

    (y_p, k_p, v_p, conv_p, h_p), (y_s, k_s, v_s, conv_s, h_s) = _trunks(prompt, sample, p)
    return (y_p, y_s, k_p, v_p, conv_p, h_p, k_s, v_s, conv_s, h_s)
```
